```python
import math
import jax
import jax.numpy as jnp
from jax import lax
import numpy as np

D_MODEL = 2048
BATCH = 4
SEQ = 2048
DEPTH = 1
DEC_BATCH = 128
DEC_SEQ = 4
PAST_LEN = 2048
PAGE_SIZE = 128

HEAD_DIM = 128
N_HEADS = D_MODEL // HEAD_DIM
N_DIFF_HEADS = N_HEADS // 2
N_SB_HEADS = N_HEADS - N_DIFF_HEADS
DIFF_DIM = HEAD_DIM // 2
ROT_DIM = DIFF_DIM // 4
ROPE_THETA = 500000.0
QUERY_BLOCK = 128
N_EXPERTS = 256
N_EXPERT_GROUPS = 8
TOPK_GROUPS = 4
TOP_K = 8
EXPERT_DIM = 512
SHARED_DIM = 512
ROUTED_SCALE = 2.5
EXPERT_BLOCK = 128
EPS = 1e-6
NEG = -1e30

kernel_name = "hybrid_diff_stickbreak_moe_step"


def rmsnorm(x, g):
    xf = x.astype(jnp.float32)
    y = xf * lax.rsqrt(jnp.mean(xf * xf, axis=-1, keepdims=True) + EPS)
    return y.astype(x.dtype) * g


def adaln(c, w, b):
    m = (jax.nn.silu(c) @ w + b)[:, None, :]
    return jnp.split(m, 6, axis=-1)


def modulate(h, shift, scale):
    return h * (1.0 + scale) + shift


def swiglu(x, w_gate, w_up, w_down):
    return (jax.nn.silu(x @ w_gate) * (x @ w_up)) @ w_down


def rope_partial(x, pos):
    half = ROT_DIM // 2
    inv_freq = ROPE_THETA ** (-jnp.arange(half, dtype=jnp.float32) / half)
    ang = pos.astype(jnp.float32)[:, None] * inv_freq[None, :]
    cos = jnp.cos(ang)[None, :, None, None, :]
    sin = jnp.sin(ang)[None, :, None, None, :]
    x1 = x[..., :half].astype(jnp.float32)
    x2 = x[..., half:ROT_DIM].astype(jnp.float32)
    rot = jnp.concatenate([x1 * cos - x2 * sin, x1 * sin + x2 * cos], axis=-1).astype(x.dtype)
    return jnp.concatenate([rot, x[..., ROT_DIM:]], axis=-1)


def project(h, w_in_l, pos):
    B, T, _ = h.shape
    qkv = (h @ w_in_l).reshape(B, T, 3, N_HEADS, HEAD_DIM)
    q, k, v = qkv[:, :, 0], qkv[:, :, 1], qkv[:, :, 2]

    def rot_diff(t):
        td = t[:, :, :N_DIFF_HEADS].reshape(B, T, N_DIFF_HEADS, 2, DIFF_DIM)
        td = rope_partial(td, pos).reshape(B, T, N_DIFF_HEADS, HEAD_DIM)
        return jnp.concatenate([td, t[:, :, N_DIFF_HEADS:]], axis=2)

    return rot_diff(q), rot_diff(k), v


def diff_attention(q, k, v, q_pos, k_pos, lam, lam_init, subln_g):
    B, Tq, H, _ = q.shape
    Tk = k.shape[1]
    q2 = q.reshape(B, Tq, H, 2, DIFF_DIM)
    k2 = k.reshape(B, Tk, H, 2, DIFF_DIM)
    s = jnp.einsum('bqhmd,bkhmd->bhmqk', q2, k2).astype(jnp.float32) * (DIFF_DIM ** -0.5)
    causal = k_pos[None, :] <= q_pos[:, None]
    p = jax.nn.softmax(jnp.where(causal, s, NEG), axis=-1)
    a = p[:, :, 0] - lam * p[:, :, 1]
    o = jnp.einsum('bhqk,bkhd->bqhd', a.astype(v.dtype), v)
    return rmsnorm(o, subln_g) * (1.0 - lam_init)


def stick_breaking(q, k, v, q_pos, k_pos):
    z = jnp.einsum('bqhd,bkhd->bhqk', q, k).astype(jnp.float32) * (HEAD_DIM ** -0.5)
    strict = k_pos[None, :] < q_pos[:, None]
    log_beta = jax.nn.log_sigmoid(z)
    log_1m = jnp.where(strict, jax.nn.log_sigmoid(-z), 0.0)
    tail = lax.cumsum(log_1m, axis=3, reverse=True) - log_1m
    a = jnp.where(strict, jnp.exp(log_beta + tail), 0.0)
    return jnp.einsum('bhqk,bkhd->bqhd', a.astype(v.dtype), v)


def mix_core(q, k, v, q_pos, k_pos, lam, lam_init, subln_g):
    o_diff = diff_attention(q[:, :, :N_DIFF_HEADS], k[:, :, :N_DIFF_HEADS], v[:, :, :N_DIFF_HEADS],
                            q_pos, k_pos, lam, lam_init, subln_g)
    o_sb = stick_breaking(q[:, :, N_DIFF_HEADS:], k[:, :, N_DIFF_HEADS:], v[:, :, N_DIFF_HEADS:], q_pos, k_pos)
    return jnp.concatenate([o_diff, o_sb], axis=2)


def prompt_mix(q, k, v, lam, lam_init, subln_g):
    B, S, H, D = q.shape
    n_blk = S // QUERY_BLOCK
    k_pos = jnp.arange(S, dtype=jnp.int32)
    q_blocks = q.reshape(B, n_blk, QUERY_BLOCK, H, D).swapaxes(0, 1)
    pos_blocks = k_pos.reshape(n_blk, QUERY_BLOCK)
    out = lax.map(lambda a: mix_core(a[0], k, v, a[1], k_pos, lam, lam_init, subln_g), (q_blocks, pos_blocks))
    return out.swapaxes(0, 1).reshape(B, S, H, D)


def sample_mix(q, k_new, v_new, cache_k, cache_v, page_table, l, lam, lam_init, subln_g):
    n_pages = page_table.shape[1]
    past = n_pages * PAGE_SIZE
    t_new = q.shape[1]
    q_pos = past + jnp.arange(t_new, dtype=jnp.int32)
    k_pos = jnp.arange(past + t_new, dtype=jnp.int32)

    def one(a):
        qi, ki, vi, pt = a
        k_all = jnp.concatenate([cache_k[l, pt].reshape(past, N_HEADS, HEAD_DIM).astype(ki.dtype), ki], axis=0)
        v_all = jnp.concatenate([cache_v[l, pt].reshape(past, N_HEADS, HEAD_DIM).astype(vi.dtype), vi], axis=0)
        return mix_core(qi[None], k_all[None], v_all[None], q_pos, k_pos, lam, lam_init, subln_g)[0]

    return lax.map(one, (q, k_new, v_new, page_table))


def routed_experts(x, idx, gate, l, w_gate_e, w_up_e, w_down_e):
    n, d = x.shape
    n_slots = n * TOP_K
    flat_e = idx.reshape(-1)
    order = jnp.argsort(flat_e)
    e_sorted = flat_e[order]
    tok_sorted = (order // TOP_K).astype(jnp.int32)
    gate_sorted = gate.reshape(-1)[order]
    counts = jnp.bincount(flat_e, length=N_EXPERTS)
    padded = (counts + EXPERT_BLOCK - 1) // EXPERT_BLOCK * EXPERT_BLOCK
    pad_end = jnp.cumsum(padded)
    pad_start = pad_end - padded
    start = jnp.cumsum(counts) - counts
    dest = pad_start[e_sorted] + jnp.arange(n_slots) - start[e_sorted]
    n_blocks = -(-n_slots // EXPERT_BLOCK) + N_EXPERTS
    n_rows = n_blocks * EXPERT_BLOCK
    row_tok = jnp.full((n_rows,), n, dtype=jnp.int32).at[dest].set(tok_sorted)
    row_gate = jnp.zeros((n_rows,), x.dtype).at[dest].set(gate_sorted)
    block_expert = jnp.minimum(
        jnp.searchsorted(pad_end, jnp.arange(n_blocks) * EXPERT_BLOCK, side='right'), N_EXPERTS - 1)
    x_pad = jnp.concatenate([x, jnp.zeros((1, d), x.dtype)], axis=0)

    def run_block(args):
        e, toks, gts = args
        xb = x_pad[toks]
        return swiglu(xb, w_gate_e[l, e], w_up_e[l, e], w_down_e[l, e]) * gts[:, None]

    y_rows = lax.map(run_block, (block_expert, row_tok.reshape(n_blocks, EXPERT_BLOCK),
                                 row_gate.reshape(n_blocks, EXPERT_BLOCK)))
    return jax.ops.segment_sum(y_rows.reshape(n_rows, d), row_tok, num_segments=n + 1)[:n]


def moe(x, l, w_router, router_bias, w_gate_e, w_up_e, w_down_e, w_gate_sh, w_up_sh, w_down_sh):
    B, T, D = x.shape
    n = B * T
    xt = x.reshape(n, D)
    scores = jax.nn.sigmoid((xt @ w_router[l]).astype(jnp.float32))
    choice = scores + router_bias[l].astype(jnp.float32)
    per_group = N_EXPERTS // N_EXPERT_GROUPS
    grp_score = lax.top_k(choice.reshape(n, N_EXPERT_GROUPS, per_group), 2)[0].sum(-1)
    top_grp = lax.top_k(grp_score, TOPK_GROUPS)[1]
    grp_mask = jnp.any(top_grp[:, :, None] == jnp.arange(N_EXPERT_GROUPS)[None, None, :], axis=1)
    choice = jnp.where(jnp.repeat(grp_mask, per_group, axis=1), choice, NEG)
    idx = lax.top_k(choice, TOP_K)[1]
    gate = jnp.take_along_axis(scores, idx, axis=1)
    gate = gate / jnp.sum(gate, axis=-1, keepdims=True) * ROUTED_SCALE
    routed = routed_experts(xt, idx, gate.astype(x.dtype), l, w_gate_e, w_up_e, w_down_e)
    shared = swiglu(xt, w_gate_sh[l], w_up_sh[l], w_down_sh[l])
    return (routed + shared).reshape(B, T, D)


def setup_inputs(seed: int = 0) -> dict:
    key = jax.random.key(seed)
    ks = jax.random.split(key, 28)
    f32 = jnp.float32
    n_pages = PAST_LEN // PAGE_SIZE
    n_used = DEC_BATCH * n_pages
    n_phys = n_used + max(n_used // 4, 1)
    perm = jax.random.permutation(ks[0], n_phys)
    page_table = perm[:n_used].reshape(DEC_BATCH, n_pages).astype(jnp.int32)
    cache_shape = (DEPTH, n_phys, PAGE_SIZE, N_HEADS, HEAD_DIM)
    inner = N_HEADS * HEAD_DIM

    def nrm(k, shape, scale):
        return jax.random.normal(k, shape, f32) * scale

    return {
        "x_prompt": nrm(ks[1], (BATCH, SEQ, D_MODEL), 1.0),
        "x_sample": nrm(ks[2], (DEC_BATCH, DEC_SEQ, D_MODEL), 1.0),
        "cache_k": nrm(ks[3], cache_shape, 1.0),
        "cache_v": nrm(ks[4], cache_shape, 1.0),
        "page_table": page_table,
        "c_prompt": nrm(ks[5], (BATCH, D_MODEL), 1.0),
        "c_sample": nrm(ks[6], (DEC_BATCH, D_MODEL), 1.0),
        "w_ada": nrm(ks[7], (DEPTH, D_MODEL, 6 * D_MODEL), 0.5 * D_MODEL ** -0.5),
        "b_ada": nrm(ks[8], (DEPTH, 6 * D_MODEL), 0.02),
        "norm1_g": 1.0 + nrm(ks[9], (DEPTH, D_MODEL), 0.02),
        "w_in": nrm(ks[10], (DEPTH, D_MODEL, 3 * inner), D_MODEL ** -0.5),
        "lambda_q1": nrm(ks[11], (DEPTH, DIFF_DIM), 0.1),
        "lambda_k1": nrm(ks[12], (DEPTH, DIFF_DIM), 0.1),
        "lambda_q2": nrm(ks[13], (DEPTH, DIFF_DIM), 0.1),
        "lambda_k2": nrm(ks[14], (DEPTH, DIFF_DIM), 0.1),
        "subln_g": 1.0 + nrm(ks[15], (DEPTH, HEAD_DIM), 0.02),
        "w_out": nrm(ks[16], (DEPTH, inner, D_MODEL), inner ** -0.5),
        "norm2_g": 1.0 + nrm(ks[17], (DEPTH, D_MODEL), 0.02),
        "w_router": nrm(ks[18], (DEPTH, D_MODEL, N_EXPERTS), D_MODEL ** -0.5),
        "router_bias": nrm(ks[19], (DEPTH, N_EXPERTS), 0.01),
        "w_gate_e": nrm(ks[20], (DEPTH, N_EXPERTS, D_MODEL, EXPERT_DIM), D_MODEL ** -0.5),
        "w_up_e": nrm(ks[21], (DEPTH, N_EXPERTS, D_MODEL, EXPERT_DIM), D_MODEL ** -0.5),
        "w_down_e": nrm(ks[22], (DEPTH, N_EXPERTS, EXPERT_DIM, D_MODEL), EXPERT_DIM ** -0.5),
        "w_gate_sh": nrm(ks[23], (DEPTH, D_MODEL, SHARED_DIM), D_MODEL ** -0.5),
        "w_up_sh": nrm(ks[24], (DEPTH, D_MODEL, SHARED_DIM), D_MODEL ** -0.5),
        "w_down_sh": nrm(ks[25], (DEPTH, SHARED_DIM, D_MODEL), SHARED_DIM ** -0.5),
        "final_g": 1.0 + nrm(ks[26], (D_MODEL,), 0.02),
    }


def reference(x_prompt, x_sample, cache_k, cache_v, page_table, c_prompt, c_sample,
              w_ada, b_ada, norm1_g, w_in, lambda_q1, lambda_k1, lambda_q2, lambda_k2,
              subln_g, w_out, norm2_g, w_router, router_bias, w_gate_e, w_up_e, w_down_e,
              w_gate_sh, w_up_sh, w_down_sh, final_g):
    f32 = jnp.float32
    B, S, _ = x_prompt.shape
    Bs, Ts, _ = x_sample.shape
    past = page_table.shape[1] * PAGE_SIZE
    pos_p = jnp.arange(S, dtype=jnp.int32)
    pos_s = past + jnp.arange(Ts, dtype=jnp.int32)
    moe_w = (w_router, router_bias, w_gate_e, w_up_e, w_down_e, w_gate_sh, w_up_sh, w_down_sh)
    xp, xs = x_prompt, x_sample
    kp_rows, vp_rows, ks_rows, vs_rows = [], [], [], []
    for l in range(DEPTH):
        lam_init = 0.8 - 0.6 * math.exp(-0.3 * l)
        lam = (jnp.exp(jnp.sum(lambda_q1[l].astype(f32) * lambda_k1[l].astype(f32)))
               - jnp.exp(jnp.sum(lambda_q2[l].astype(f32) * lambda_k2[l].astype(f32))) + lam_init)
        sh1p, sc1p, g1p, sh2p, sc2p, g2p = adaln(c_prompt, w_ada[l], b_ada[l])
        sh1s, sc1s, g1s, sh2s, sc2s, g2s = adaln(c_sample, w_ada[l], b_ada[l])

        hp = modulate(rmsnorm(xp, norm1_g[l]), sh1p, sc1p)
        qp, kp, vp = project(hp, w_in[l], pos_p)
        op = prompt_mix(qp, kp, vp, lam, lam_init, subln_g[l])
        xp = xp + g1p * (op.reshape(B, S, N_HEADS * HEAD_DIM) @ w_out[l])
        hp = modulate(rmsnorm(xp, norm2_g[l]), sh2p, sc2p)
        xp = xp + g2p * moe(hp, l, *moe_w)

        hs = modulate(rmsnorm(xs, norm1_g[l]), sh1s, sc1s)
        qs, k_s, v_s = project(hs, w_in[l], pos_s)
        o_s = sample_mix(qs, k_s, v_s, cache_k, cache_v, page_table, l, lam, lam_init, subln_g[l])
        xs = xs + g1s * (o_s.reshape(Bs, Ts, N_HEADS * HEAD_DIM) @ w_out[l])
        hs = modulate(rmsnorm(xs, norm2_g[l]), sh2s, sc2s)
        xs = xs + g2s * moe(hs, l, *moe_w)

        kp_rows.append(kp)
        vp_rows.append(vp)
        ks_rows.append(k_s)
        vs_rows.append(v_s)

    y_prompt = rmsnorm(xp, final_g)
    y_sample = rmsnorm(xs, final_g)
    k_prompt_new = jnp.stack(kp_rows)
    v_prompt_new = jnp.stack(vp_rows)
    k_sample_new = jnp.stack(ks_rows)
    v_sample_new = jnp.stack(vs_rows)
    return (y_prompt, y_sample, k_prompt_new, v_prompt_new, k_sample_new, v_sample_new)
```

```python
import functools
import math

import jax
import jax.numpy as jnp
from jax import lax
from jax.experimental import pallas as pl
from jax.experimental.pallas import tpu as pltpu

F32 = jnp.float32
BF16 = jnp.bfloat16

HEAD_DIM = 128
DIFF_DIM = HEAD_DIM // 2
ROT_DIM = DIFF_DIM // 4
ROT_HALF = ROT_DIM // 2
ROPE_THETA = 500000.0
N_EXPERT_GROUPS = 8
TOPK_GROUPS = 4
TOP_K = 8
ROUTED_SCALE = 2.5
EPS = 1e-6
NEG = -1e30
EXPERT_ROWS = 128
VMEM_LIMIT = 56 * 1024 * 1024


def _params(*sem):
    return pltpu.CompilerParams(dimension_semantics=sem, vmem_limit_bytes=VMEM_LIMIT)


def _mod(ref):
    return ref[0] if len(ref.shape) == 3 else ref[...]


def _rms(x):
    return x * lax.rsqrt(jnp.mean(x * x, axis=-1, keepdims=True) + EPS)


def _adaln_kernel(c_ref, w_ref, b_ref, o_ref):
    c = c_ref[...]
    s = (c * jax.nn.sigmoid(c)).astype(BF16)
    o_ref[...] = jnp.dot(s, w_ref[...].astype(BF16), preferred_element_type=F32) + b_ref[...]


def _adaln(c, w, b):
    m, d = c.shape
    n = w.shape[1]
    tn = _tile(n, 1024, HEAD_DIM)
    return pl.pallas_call(
        _adaln_kernel,
        grid=(n // tn,),
        in_specs=[pl.BlockSpec((m, d), lambda j: (0, 0)),
                  pl.BlockSpec((d, tn), lambda j: (0, j)),
                  pl.BlockSpec((1, tn), lambda j: (0, j))],
        out_specs=pl.BlockSpec((m, tn), lambda j: (0, j)),
        out_shape=jax.ShapeDtypeStruct((m, n), F32),
        compiler_params=_params("arbitrary"),
        name="adaln",
    )(c, w, b.reshape(1, n))


def _norm_mod_kernel(x_ref, g_ref, sh_ref, sc_ref, h_ref):
    y = _rms(x_ref[...]) * g_ref[...]
    h_ref[...] = (y * (1.0 + _mod(sc_ref)) + _mod(sh_ref)).astype(h_ref.dtype)


def _mod_spec(arr, tm, rows_per_group, d):
    if arr.ndim == 3:
        return pl.BlockSpec((1, 1, d), lambda i: (i * tm // rows_per_group, 0, 0))
    return pl.BlockSpec((tm, d), lambda i: (i, 0))


def _norm_mod(x, g, shift, scale, rows_per_group, tm):
    m, d = x.shape
    return pl.pallas_call(
        _norm_mod_kernel,
        grid=(m // tm,),
        in_specs=[pl.BlockSpec((tm, d), lambda i: (i, 0)),
                  pl.BlockSpec((1, d), lambda i: (0, 0)),
                  _mod_spec(shift, tm, rows_per_group, d),
                  _mod_spec(scale, tm, rows_per_group, d)],
        out_specs=pl.BlockSpec((tm, d), lambda i: (i, 0)),
        out_shape=jax.ShapeDtypeStruct((m, d), BF16),
        compiler_params=_params("arbitrary"),
        name="norm_mod",
    )(x, g.reshape(1, d), shift, scale)


def _rope(x, cos, sa, sb):
    return x * cos + pltpu.roll(x, HEAD_DIM - ROT_HALF, 1) * sa + pltpu.roll(x, ROT_HALF, 1) * sb


def _proj_kernel(h_ref, w_ref, cos_ref, sa_ref, sb_ref, *o_refs, rope, diff_scale):
    acc = jnp.dot(h_ref[...], w_ref[...], preferred_element_type=F32)
    j = pl.program_id(0)

    def store(val):
        for o in o_refs:
            o[...] = val.astype(o.dtype)

    @pl.when(j == 0)
    def _():
        if rope:
            cos, sa, sb = cos_ref[...], sa_ref[...], sb_ref[...]
            blocks = [_rope(acc[:, c:c + HEAD_DIM], cos, sa, sb) * diff_scale
                      for c in range(0, acc.shape[1], HEAD_DIM)]
            store(jnp.concatenate(blocks, axis=1))
        else:
            store(acc)

    @pl.when(j != 0)
    def _():
        store(acc)


def _proj(h, w, col0, n, tables, out_dtypes, rope, diff_scale, tm):
    m, d = h.shape
    tn = n // 2
    cos, sa, sb = tables
    n_tab = cos.shape[0] // tm
    tab_spec = pl.BlockSpec((tm, HEAD_DIM), lambda j, i: (i % n_tab, 0))
    off = col0 // tn
    return pl.pallas_call(
        functools.partial(_proj_kernel, rope=rope, diff_scale=diff_scale),
        grid=(2, m // tm),
        in_specs=[pl.BlockSpec((tm, d), lambda j, i: (i, 0)),
                  pl.BlockSpec((d, tn), lambda j, i: (0, off + j)),
                  tab_spec, tab_spec, tab_spec],
        out_specs=[pl.BlockSpec((tm, tn), lambda j, i: (i, j)) for _ in out_dtypes],
        out_shape=[jax.ShapeDtypeStruct((m, n), dt) for dt in out_dtypes],
        compiler_params=_params("arbitrary", "arbitrary"),
        name="proj",
    )(h, w, cos, sa, sb)


def _rope_tables(pos):
    inv_freq = ROPE_THETA ** (-jnp.arange(ROT_HALF, dtype=F32) / ROT_HALF)
    ang = pos.astype(F32)[:, None] * inv_freq[None, :]
    cos, sin = jnp.cos(ang), jnp.sin(ang)
    t = pos.shape[0]
    one = jnp.ones((t, DIFF_DIM - ROT_DIM), F32)
    zero = jnp.zeros((t, DIFF_DIM - ROT_DIM), F32)
    z8 = jnp.zeros((t, ROT_HALF), F32)
    cos_l = jnp.concatenate([cos, cos, one], axis=1)
    sa_l = jnp.concatenate([-sin, z8, zero], axis=1)
    sb_l = jnp.concatenate([z8, sin, zero], axis=1)
    return tuple(jnp.concatenate([a, a], axis=1) for a in (cos_l, sa_l, sb_l))


def _split_maps(q):
    lane = lax.broadcasted_iota(jnp.int32, q.shape, 1)
    zero = jnp.zeros_like(q)
    return jnp.concatenate([jnp.where(lane < DIFF_DIM, q, zero),
                            jnp.where(lane >= DIFF_DIM, q, zero)], axis=0)


def _diff_attn_kernel(lam_ref, q_ref, k_ref, v_ref, g_ref, o_ref, m_sc, l_sc, acc_sc, *, t, out_scale):
    i = pl.program_id(2)
    q2 = _split_maps(q_ref[...])
    m_sc[...] = jnp.full(m_sc.shape, NEG, F32)
    l_sc[...] = jnp.zeros(l_sc.shape, F32)
    acc_sc[...] = jnp.zeros(acc_sc.shape, F32)

    def step(j, masked):
        start = pl.multiple_of(j * t, t)
        kj = k_ref[pl.ds(start, t), :]
        vj = v_ref[pl.ds(start, t), :]
        s = lax.dot_general(q2, kj, (((1,), (1,)), ((), ())), preferred_element_type=F32)
        if masked:
            row = lax.broadcasted_iota(jnp.int32, (2 * t, t), 0)
            row = jnp.where(row >= t, row - t, row)
            col = lax.broadcasted_iota(jnp.int32, (2 * t, t), 1)
            s = jnp.where(col <= row, s, NEG)
        m_old = m_sc[...]
        m_new = jnp.maximum(m_old, jnp.max(s, axis=-1, keepdims=True))
        alpha = jnp.exp(m_old - m_new)
        p = jnp.exp(s - m_new)
        l_sc[...] = alpha * l_sc[...] + jnp.sum(p, axis=-1, keepdims=True)
        acc_sc[...] = alpha * acc_sc[...] + jnp.dot(p.astype(BF16), vj, preferred_element_type=F32)
        m_sc[...] = m_new

    def body(j, c):
        step(j, False)
        return c

    lax.fori_loop(0, i, body, 0)
    step(i, True)

    o2 = acc_sc[...] / l_sc[...]
    o = o2[:t] - lam_ref[0] * o2[t:]
    o_ref[...] = (_rms(o) * g_ref[...] * out_scale).astype(o_ref.dtype)


def _sb_weights(z, strict):
    soft = jnp.log(1.0 + jnp.exp(-jnp.abs(z)))
    log_beta = jnp.minimum(z, 0.0) - soft
    log_1m = jnp.minimum(-z, 0.0) - soft
    if strict is not None:
        log_1m = jnp.where(strict, log_1m, 0.0)
    return log_beta, log_1m


def _later_sum(x, upper):
    hi = x.astype(BF16)
    lo = (x - hi.astype(F32)).astype(BF16)
    return (jnp.dot(hi, upper, preferred_element_type=F32)
            + jnp.dot(lo, upper, preferred_element_type=F32))


def _upper(t):
    r = lax.broadcasted_iota(jnp.int32, (t, t), 0)
    c = lax.broadcasted_iota(jnp.int32, (t, t), 1)
    return jnp.where(r > c, 1.0, 0.0).astype(BF16)


def _sb_attn_kernel(q_ref, k_ref, v_ref, o_ref, c_sc, acc_sc, *, t, scale):
    i = pl.program_id(2)
    q = q_ref[...]
    upper = _upper(t)
    c_sc[...] = jnp.zeros(c_sc.shape, F32)
    acc_sc[...] = jnp.zeros(acc_sc.shape, F32)

    def step(j, masked):
        start = pl.multiple_of(j * t, t)
        kj = k_ref[pl.ds(start, t), :]
        vj = v_ref[pl.ds(start, t), :]
        z = lax.dot_general(q, kj, (((1,), (1,)), ((), ())), preferred_element_type=F32) * scale
        strict = None
        if masked:
            row = lax.broadcasted_iota(jnp.int32, (t, t), 0)
            col = lax.broadcasted_iota(jnp.int32, (t, t), 1)
            strict = col < row
        log_beta, log_1m = _sb_weights(z, strict)
        a = jnp.exp(log_beta + _later_sum(log_1m, upper) + c_sc[...])
        if masked:
            a = jnp.where(strict, a, 0.0)
        acc_sc[...] += jnp.dot(a.astype(BF16), vj, preferred_element_type=F32)
        c_sc[...] += jnp.sum(log_1m, axis=-1, keepdims=True)

    step(i, True)

    def body(r, c):
        step(i - 1 - r, False)
        return c

    lax.fori_loop(0, i, body, 0)
    o_ref[...] = acc_sc[...].astype(o_ref.dtype)


def _prompt_attention(q, k, v, lam, subln_g, batch, seq, n_diff, lam_init, t):
    m, d = q.shape
    n_heads = d // HEAD_DIM
    n_sb = n_heads - n_diff
    nq = seq // t
    params = _params("arbitrary", "arbitrary", "arbitrary")

    def q_spec(h0):
        return pl.BlockSpec((t, HEAD_DIM), lambda b, h, i: (b * nq + i, h0 + h))

    def kv_spec(h0):
        return pl.BlockSpec((seq, HEAD_DIM), lambda b, h, i: (b, h0 + h))

    o_spec = pl.BlockSpec((t, HEAD_DIM), lambda b, h, i: (b * nq + i, h))
    o_diff = pl.pallas_call(
        functools.partial(_diff_attn_kernel, t=t, out_scale=1.0 - lam_init),
        grid=(batch, n_diff, nq),
        in_specs=[pl.BlockSpec(memory_space=pltpu.SMEM), q_spec(0), kv_spec(0), kv_spec(0),
                  pl.BlockSpec((1, HEAD_DIM), lambda b, h, i: (0, 0))],
        out_specs=o_spec,
        out_shape=jax.ShapeDtypeStruct((m, n_diff * HEAD_DIM), BF16),
        scratch_shapes=[pltpu.VMEM((2 * t, 1), F32), pltpu.VMEM((2 * t, 1), F32),
                        pltpu.VMEM((2 * t, HEAD_DIM), F32)],
        compiler_params=params,
        name="diff_attn",
    )(lam, q, k, v, subln_g.reshape(1, HEAD_DIM))
    o_sb = pl.pallas_call(
        functools.partial(_sb_attn_kernel, t=t, scale=HEAD_DIM ** -0.5),
        grid=(batch, n_sb, nq),
        in_specs=[q_spec(n_diff), kv_spec(n_diff), kv_spec(n_diff)],
        out_specs=o_spec,
        out_shape=jax.ShapeDtypeStruct((m, n_sb * HEAD_DIM), BF16),
        scratch_shapes=[pltpu.VMEM((t, 1), F32), pltpu.VMEM((t, HEAD_DIM), F32)],
        compiler_params=params,
        name="sb_attn",
    )(q, k, v)
    return o_diff, o_sb


def _sample_attn_kernel(pt_ref, lam_ref, q_ref, kn_ref, vn_ref, *rest, n_heads, n_diff, t_new, n_par,
                        last_step, sb_scale, out_scale):
    k_refs, v_refs = rest[:n_par], rest[n_par:2 * n_par]
    g_ref, o_ref, qm_sc, m_sc, l_sc, acc_sc = rest[2 * n_par:]
    p = pl.program_id(1)
    half = 4
    rows = 2 * half
    row = lax.broadcasted_iota(jnp.int32, (rows, 1), 0)
    t_row = row % half
    page = k_refs[0].shape[0] // n_heads
    upper = _upper(page)

    def head_rows(ref, h):
        return ref[pl.ds(h, page, stride=n_heads), :].astype(BF16)
    nd8 = rows * n_diff
    ns8 = rows * (n_heads - n_diff)

    def cols(h):
        return slice(h * HEAD_DIM, (h + 1) * HEAD_DIM)

    def hrows(h):
        return slice(h * rows, (h + 1) * rows)

    @pl.when(p == 0)
    def _init():
        pad = jnp.zeros((half - t_new, HEAD_DIM), F32)
        for h in range(n_heads):
            qh = jnp.concatenate([q_ref[0, :, cols(h)], pad], axis=0) if t_new < half else q_ref[0, :, cols(h)]
            if h < n_diff:
                lane = lax.broadcasted_iota(jnp.int32, qh.shape, 1)
                qm = jnp.concatenate([jnp.where(lane < DIFF_DIM, qh, 0.0),
                                      jnp.where(lane >= DIFF_DIM, qh, 0.0)], axis=0)
            else:
                qm = jnp.concatenate([qh, jnp.zeros_like(qh)], axis=0)
            qm_sc[h] = qm
            kn = kn_ref[0, :, cols(h)]
            vn = vn_ref[0, :, cols(h)]
            s = [jnp.sum(qm * kn[j:j + 1, :], axis=-1, keepdims=True) for j in range(t_new)]
            if h < n_diff:
                ok = [t_row >= j for j in range(t_new)]
                m = functools.reduce(jnp.maximum, [jnp.where(ok[j], s[j], NEG) for j in range(t_new)])
                pj = [jnp.where(ok[j], jnp.exp(s[j] - m), 0.0) for j in range(t_new)]
                m_sc[hrows(h)] = m
                l_sc[hrows(h)] = sum(pj)
                acc_sc[hrows(h)] = sum(pj[j] * vn[j:j + 1, :] for j in range(t_new))
            else:
                strict = [t_row > j for j in range(t_new)]
                lb, l1 = zip(*[_sb_weights(s[j] * sb_scale, strict[j]) for j in range(t_new)])
                acc = jnp.zeros((rows, HEAD_DIM), F32)
                later = jnp.zeros((rows, 1), F32)
                for j in reversed(range(t_new)):
                    acc = acc + jnp.where(strict[j], jnp.exp(lb[j] + later), 0.0) * vn[j:j + 1, :]
                    later = later + l1[j]
                m_sc[hrows(h)] = later
                l_sc[hrows(h)] = jnp.ones((rows, 1), F32)
                acc_sc[hrows(h)] = acc

    nt = (((1,), (1,)), ((), ()))
    s = jnp.concatenate(
        [jnp.concatenate([lax.dot_general(qm_sc[h].astype(BF16), head_rows(k_refs[a], h), nt,
                                          preferred_element_type=F32) for h in range(n_heads)], axis=0)
         for a in range(n_par)], axis=1)

    m_old = m_sc[:nd8]
    m_new = jnp.maximum(m_old, jnp.max(s[:nd8], axis=-1, keepdims=True))
    alpha = jnp.exp(m_old - m_new)
    p_diff = jnp.exp(s[:nd8] - m_new)
    l_sc[:nd8] = alpha * l_sc[:nd8] + jnp.sum(p_diff, axis=-1, keepdims=True)
    m_sc[:nd8] = m_new

    log_beta, log_1m = _sb_weights(s[nd8:] * sb_scale, None)
    chunks = [log_1m[:, a * page:(a + 1) * page] for a in range(n_par)]
    within = _later_sum(jnp.concatenate(chunks, axis=0), upper)
    later = m_sc[nd8:]
    tails = [None] * n_par
    for a in reversed(range(n_par)):
        tails[a] = within[a * ns8:(a + 1) * ns8] + later
        later = later + jnp.sum(chunks[a], axis=-1, keepdims=True)
    m_sc[nd8:] = later
    p_sb = jnp.exp(log_beta + jnp.concatenate(tails, axis=1))

    pm = jnp.concatenate([p_diff, p_sb], axis=0)
    pv = []
    for h in range(n_heads):
        parts = [jnp.dot(pm[hrows(h), a * page:(a + 1) * page].astype(BF16),
                         head_rows(v_refs[a], h), preferred_element_type=F32)
                 for a in range(n_par)]
        pv.append(functools.reduce(lambda x, y: x + y, parts))
    decay = jnp.concatenate([alpha, jnp.ones((ns8, 1), F32)], axis=0)
    acc_sc[...] = decay * acc_sc[...] + jnp.concatenate(pv, axis=0)

    @pl.when(p == last_step)
    def _fin():
        for h in range(n_heads):
            if h < n_diff:
                o2 = acc_sc[hrows(h)] / l_sc[hrows(h)]
                o = o2[:half] - lam_ref[0] * o2[half:]
                o = _rms(o) * g_ref[...] * out_scale
            else:
                o = acc_sc[hrows(h)][:half]
            o_ref[0, :, cols(h)] = o[:t_new]


def _sample_attention(l, q, k_new, v_new, cache_k, cache_v, page_table, lam, subln_g, n_diff, lam_init):
    n_seq, t_new, d = q.shape
    n_heads = d // HEAD_DIM
    n_pages = page_table.shape[1]
    n_phys, page = cache_k.shape[1:3]
    assert t_new <= 4
    n_par = _tile(n_pages, 4, 1)
    n_steps = n_pages // n_par
    new_spec = pl.BlockSpec((1, t_new, d), lambda b, p, pt: (b, 0, 0))
    cache_k = cache_k.reshape(-1, HEAD_DIM)
    cache_v = cache_v.reshape(-1, HEAD_DIM)

    def cache_spec(a):
        return pl.BlockSpec((page * n_heads, HEAD_DIM),
                            lambda b, p, pt: (l * n_phys + pt[b, n_pages - (p + 1) * n_par + a], 0))

    cache_specs = [cache_spec(a) for a in range(n_par)]
    rows = 8 * n_heads
    return pl.pallas_call(
        functools.partial(_sample_attn_kernel, n_heads=n_heads, n_diff=n_diff, t_new=t_new, n_par=n_par,
                          last_step=n_steps - 1, sb_scale=HEAD_DIM ** -0.5, out_scale=1.0 - lam_init),
        grid_spec=pltpu.PrefetchScalarGridSpec(
            num_scalar_prefetch=1,
            grid=(n_seq, n_steps),
            in_specs=[pl.BlockSpec(memory_space=pltpu.SMEM), new_spec, new_spec, new_spec,
                      *cache_specs, *cache_specs,
                      pl.BlockSpec((1, HEAD_DIM), lambda b, p, pt: (0, 0))],
            out_specs=new_spec,
            scratch_shapes=[pltpu.VMEM((n_heads, 8, HEAD_DIM), F32), pltpu.VMEM((rows, 1), F32),
                            pltpu.VMEM((rows, 1), F32), pltpu.VMEM((rows, HEAD_DIM), F32)]),
        out_shape=jax.ShapeDtypeStruct((n_seq, t_new, d), F32),
        compiler_params=_params("arbitrary", "arbitrary"),
        name="sample_attn",
    )(page_table, lam, q, k_new, v_new, *([cache_k] * n_par), *([cache_v] * n_par),
      subln_g.reshape(1, HEAD_DIM))


def _split_bf16(x):
    hi = x.astype(BF16)
    return hi, (x - hi.astype(F32)).astype(BF16)


def _outproj_kernel(oa_ref, ob_ref, wa_ref, wb_ref, x_ref, g1_ref, n2_ref, sh_ref, sc_ref,
                    rhi_ref, rlo_ref, x1_ref, h2_ref, lg_ref):
    attn = (jnp.dot(oa_ref[...].astype(BF16), wa_ref[...], preferred_element_type=F32)
            + jnp.dot(ob_ref[...].astype(BF16), wb_ref[...], preferred_element_type=F32))
    x1 = x_ref[...] + _mod(g1_ref) * attn
    x1_ref[...] = x1
    h2 = _rms(x1) * n2_ref[...] * (1.0 + _mod(sc_ref)) + _mod(sh_ref)
    hi, lo = _split_bf16(h2)
    h2_ref[...] = hi.astype(F32)
    lg_ref[...] = (jnp.dot(hi, rhi_ref[...], preferred_element_type=F32)
                   + jnp.dot(lo, rhi_ref[...], preferred_element_type=F32)
                   + jnp.dot(hi, rlo_ref[...], preferred_element_type=F32))


def _outproj(o_a, o_b, blk_b, w_out, x, g1, norm2_g, sh2, sc2, r_hi, r_lo, rows_per_group, tm):
    m, d = x.shape
    dh = d // 2
    n_e = r_hi.shape[1]
    row = lambda i: (i, 0)
    const = lambda i: (0, 0)
    return pl.pallas_call(
        _outproj_kernel,
        grid=(m // tm,),
        in_specs=[pl.BlockSpec((tm, dh), row),
                  pl.BlockSpec((tm, dh), lambda i: (i, blk_b)),
                  pl.BlockSpec((dh, d), const),
                  pl.BlockSpec((dh, d), lambda i: (1, 0)),
                  pl.BlockSpec((tm, d), row),
                  _mod_spec(g1, tm, rows_per_group, d),
                  pl.BlockSpec((1, d), const),
                  _mod_spec(sh2, tm, rows_per_group, d),
                  _mod_spec(sc2, tm, rows_per_group, d),
                  pl.BlockSpec((d, n_e), const),
                  pl.BlockSpec((d, n_e), const)],
        out_specs=[pl.BlockSpec((tm, d), row), pl.BlockSpec((tm, d), row), pl.BlockSpec((tm, n_e), row)],
        out_shape=[jax.ShapeDtypeStruct((m, d), F32), jax.ShapeDtypeStruct((m, d), F32),
                   jax.ShapeDtypeStruct((m, n_e), F32)],
        compiler_params=_params("arbitrary"),
        name="outproj",
    )(o_a, o_b, w_out, w_out, x, g1, norm2_g.reshape(1, d), sh2, sc2, r_hi, r_lo)


def _route_kernel(lg_ref, bias_ref, idx_ref, gate_ref, rank_ref, cnt_ref, cnt_sc):
    @pl.when(pl.program_id(0) == 0)
    def _():
        cnt_sc[...] = jnp.zeros(cnt_sc.shape, F32)

    scores = jax.nn.sigmoid(lg_ref[...]).T
    n_e, tm = scores.shape
    per = n_e // N_EXPERT_GROUPS
    choice = scores + bias_ref[...]
    ninf = -jnp.inf
    sub = lax.broadcasted_iota(jnp.int32, (per, tm), 0).astype(F32)
    grp = []
    for g in range(N_EXPERT_GROUPS):
        cg = choice[g * per:(g + 1) * per]
        m1 = jnp.max(cg, axis=0, keepdims=True)
        i1 = jnp.min(jnp.where(cg == m1, sub, float(per)), axis=0, keepdims=True)
        m2 = jnp.max(jnp.where(sub == i1, ninf, cg), axis=0, keepdims=True)
        grp.append(m1 + m2)
    masked = []
    for g in range(N_EXPERT_GROUPS):
        beaten = sum(((grp[o] > grp[g]) if o > g else (grp[o] >= grp[g])).astype(F32)
                     for o in range(N_EXPERT_GROUPS) if o != g)
        masked.append(jnp.where(beaten < TOPK_GROUPS, choice[g * per:(g + 1) * per], NEG))
    cm = jnp.concatenate(masked, axis=0)
    eid = lax.broadcasted_iota(jnp.int32, (n_e, tm), 0).astype(F32)
    r = lax.broadcasted_iota(jnp.int32, (tm, tm), 0)
    c = lax.broadcasted_iota(jnp.int32, (tm, tm), 1)
    earlier = jnp.where(r < c, 1.0, 0.0).astype(BF16)
    count = cnt_sc[...]
    idx, gate, rank = [], [], []
    for _ in range(TOP_K):
        mx = jnp.max(cm, axis=0, keepdims=True)
        ik = jnp.min(jnp.where(cm == mx, eid, float(n_e)), axis=0, keepdims=True)
        hit = eid == ik
        gate.append(jnp.sum(jnp.where(hit, scores, 0.0), axis=0, keepdims=True))
        cm = jnp.where(hit, ninf, cm)
        idx.append(ik)
        one = jnp.where(hit, 1.0, 0.0)
        before = jnp.dot(one.astype(BF16), earlier, preferred_element_type=F32)
        rank.append(jnp.sum(jnp.where(hit, before + count, 0.0), axis=0, keepdims=True))
        count = count + jnp.sum(one, axis=1, keepdims=True)
    cnt_sc[...] = count
    cnt_ref[...] = count
    gates = jnp.concatenate(gate, axis=0)
    idx_ref[...] = jnp.concatenate(idx, axis=0).astype(jnp.int32)
    rank_ref[...] = jnp.concatenate(rank, axis=0).astype(jnp.int32)
    gate_ref[...] = gates / jnp.sum(gates, axis=0, keepdims=True) * ROUTED_SCALE


def _route(logits, bias, tm):
    m, n_e = logits.shape
    slot = pl.BlockSpec((TOP_K, tm), lambda i: (0, i))
    return pl.pallas_call(
        _route_kernel,
        grid=(m // tm,),
        in_specs=[pl.BlockSpec((tm, n_e), lambda i: (i, 0)),
                  pl.BlockSpec((n_e, 1), lambda i: (0, 0))],
        out_specs=[slot, slot, slot, pl.BlockSpec((n_e, 1), lambda i: (0, 0))],
        out_shape=[jax.ShapeDtypeStruct((TOP_K, m), jnp.int32), jax.ShapeDtypeStruct((TOP_K, m), F32),
                   jax.ShapeDtypeStruct((TOP_K, m), jnp.int32), jax.ShapeDtypeStruct((n_e, 1), F32)],
        scratch_shapes=[pltpu.VMEM((n_e, 1), F32)],
        compiler_params=_params("arbitrary"),
        name="route",
    )(logits, bias.reshape(n_e, 1))


def _dest_kernel(idx_ref, rank_ref, start_ref, dest_ref):
    idx = idx_ref[...]
    k, tm = idx.shape
    n_e = start_ref.shape[0]
    eid = lax.broadcasted_iota(jnp.int32, (n_e, tm), 0)
    base = [jnp.sum(jnp.where(eid == idx[j:j + 1, :], start_ref[...], 0.0), axis=0, keepdims=True)
            for j in range(k)]
    dest_ref[...] = jnp.concatenate(base, axis=0).astype(jnp.int32) + rank_ref[...]


def _dest(idx_t, rank_t, start, tm):
    k, m = idx_t.shape
    n_e = start.shape[0]
    slot = pl.BlockSpec((k, tm), lambda i: (0, i))
    return pl.pallas_call(
        _dest_kernel,
        grid=(m // tm,),
        in_specs=[slot, slot, pl.BlockSpec((n_e, 1), lambda i: (0, 0))],
        out_specs=slot,
        out_shape=jax.ShapeDtypeStruct((k, m), jnp.int32),
        compiler_params=_params("arbitrary"),
        name="dest",
    )(idx_t, rank_t, start)


def _expert_kernel(be_ref, ob_ref, first_ref, next_ref, slot_ref, nv_ref, x_ref, wg_hbm, wu_hbm, wd_hbm,
                   y_ref, wg_buf, wu_buf, wd_buf, wg_sc, wu_sc, wd_sc, sem):
    i = pl.program_id(0)

    def copies(e, slot):
        return (pltpu.make_async_copy(wg_hbm.at[e], wg_buf.at[slot], sem.at[slot, 0]),
                pltpu.make_async_copy(wu_hbm.at[e], wu_buf.at[slot], sem.at[slot, 1]),
                pltpu.make_async_copy(wd_hbm.at[e], wd_buf.at[slot], sem.at[slot, 2]))

    @pl.when(first_ref[i] == 1)
    def _():
        slot = slot_ref[i]

        @pl.when(i == 0)
        def _():
            for cp in copies(be_ref[i], slot):
                cp.start()

        for cp in copies(be_ref[i], slot):
            cp.wait()

        @pl.when(next_ref[i] >= 0)
        def _():
            for cp in copies(next_ref[i], 1 - slot):
                cp.start()

        wg_sc[...] = wg_buf[slot].astype(BF16)
        wu_sc[...] = wu_buf[slot].astype(BF16)
        wd_sc[...] = wd_buf[slot].astype(BF16)

    @pl.when(i < nv_ref[0])
    def _():
        x = x_ref[...].astype(BF16)
        a = jnp.dot(x, wg_sc[...], preferred_element_type=F32)
        u = jnp.dot(x, wu_sc[...], preferred_element_type=F32)
        mid = (a * jax.nn.sigmoid(a) * u).astype(BF16)
        y_ref[...] = jnp.dot(mid, wd_sc[...], preferred_element_type=F32)

    @pl.when(i >= nv_ref[0])
    def _():
        y_ref[...] = jnp.zeros(y_ref.shape, F32)


def _experts(x_rows, plan, w_gate, w_up, w_down):
    r, d = x_rows.shape
    n_e, _, f = w_gate.shape
    tm = EXPERT_ROWS
    n_blocks = r // tm
    rows = lambda i, be, ob, *_: (ob[i], 0)
    hbm = pl.BlockSpec(memory_space=pl.ANY)
    return pl.pallas_call(
        _expert_kernel,
        grid_spec=pltpu.PrefetchScalarGridSpec(
            num_scalar_prefetch=6,
            grid=(n_blocks,),
            in_specs=[pl.BlockSpec((tm, d), rows), hbm, hbm, hbm],
            out_specs=pl.BlockSpec((tm, d), lambda i, *_: (i, 0)),
            scratch_shapes=[pltpu.VMEM((2, d, f), F32), pltpu.VMEM((2, d, f), F32), pltpu.VMEM((2, f, d), F32),
                            pltpu.VMEM((d, f), BF16), pltpu.VMEM((d, f), BF16), pltpu.VMEM((f, d), BF16),
                            pltpu.SemaphoreType.DMA((2, 3))]),
        out_shape=jax.ShapeDtypeStruct((r, d), F32),
        compiler_params=_params("arbitrary"),
        name="experts",
    )(*plan, x_rows, w_gate, w_up, w_down)


def _block_plan(counts, n_slots):
    n_e = counts.shape[0]
    tm = EXPERT_ROWS
    padded = (counts + tm - 1) // tm * tm
    pad_end = jnp.cumsum(padded)
    pad_start = pad_end - padded
    n_blocks = -(-n_slots // tm) + n_e
    blk = jnp.arange(n_blocks, dtype=jnp.int32)
    block_expert = jnp.minimum(jnp.searchsorted(pad_end, blk * tm, side='right'), n_e - 1).astype(jnp.int32)
    n_valid = (pad_end[-1] // tm).astype(jnp.int32)
    out_block = jnp.minimum(blk, n_valid)
    prev = jnp.concatenate([jnp.full((1,), -1, jnp.int32), block_expert[:-1]])
    first = ((blk < n_valid) & (block_expert != prev)).astype(jnp.int32)
    used = counts > 0
    ids = jnp.arange(n_e, dtype=jnp.int32)
    later = lax.cummin(jnp.where(used, ids, n_e), axis=0, reverse=True)
    nxt = jnp.concatenate([later[1:], jnp.full((1,), n_e, jnp.int32)])
    nxt = jnp.where(nxt < n_e, nxt, -1)
    ordinal = jnp.cumsum(used.astype(jnp.int32)) - 1
    plan = (block_expert, out_block, first, nxt[block_expert], ordinal[block_expert] % 2, n_valid.reshape(1))
    return pad_start, n_blocks * tm, plan


def _final_kernel(h_ref, wg_ref, wu_ref, wd_ref, yk_ref, gt_ref, x1_ref, g2_ref, fg_ref, y_ref):
    h = h_ref[...].astype(BF16)
    a = jnp.dot(h, wg_ref[...], preferred_element_type=F32)
    u = jnp.dot(h, wu_ref[...], preferred_element_type=F32)
    mid = (a * jax.nn.sigmoid(a) * u).astype(BF16)
    moe = jnp.dot(mid, wd_ref[...], preferred_element_type=F32)
    gt = gt_ref[...]
    for j in range(yk_ref.shape[0]):
        moe = moe + yk_ref[j] * gt[:, j:j + 1]
    x2 = x1_ref[...] + _mod(g2_ref) * moe
    y_ref[...] = _rms(x2) * fg_ref[...]


def _final(h2, wg, wu, wd, y_slots, gates, row0, x1, g2, final_g, rows_per_group, tm):
    m, d = x1.shape
    f = wg.shape[1]
    k = y_slots.shape[0]
    assert row0 % tm == 0
    b0 = row0 // tm
    row = lambda i: (i, 0)
    const = lambda i: (0, 0)
    return pl.pallas_call(
        _final_kernel,
        grid=(m // tm,),
        in_specs=[pl.BlockSpec((tm, d), row), pl.BlockSpec((d, f), const), pl.BlockSpec((d, f), const),
                  pl.BlockSpec((f, d), const), pl.BlockSpec((k, tm, d), lambda i: (0, b0 + i, 0)),
                  pl.BlockSpec((tm, k), lambda i: (b0 + i, 0)), pl.BlockSpec((tm, d), row),
                  _mod_spec(g2, tm, rows_per_group, d), pl.BlockSpec((1, d), const)],
        out_specs=pl.BlockSpec((tm, d), row),
        out_shape=jax.ShapeDtypeStruct((m, d), F32),
        compiler_params=_params("arbitrary"),
        name="final",
    )(h2, wg, wu, wd, y_slots, gates, x1, g2, final_g.reshape(1, d))


def _tile(m, cap, align=8):
    for t in range(min(m, cap) // align * align, 0, -align):
        if m % t == 0:
            return t
    raise ValueError(f"no tile for {m} rows")


def _layer(l, xp, xs, cache_k, cache_v, page_table, c_all, w_ada, b_ada, norm1_g, w_in, lam, lam_init,
           subln_g, w_out, norm2_g, w_router, router_bias, w_gate_e, w_up_e, w_down_e,
           w_gate_sh, w_up_sh, w_down_sh, final_g):
    batch, seq, d = xp.shape
    n_seq, t_new, _ = xs.shape
    n_heads = d // HEAD_DIM
    n_diff = n_heads // 2
    n_p, n_s = batch * seq, n_seq * t_new
    past = page_table.shape[1] * cache_k.shape[2]

    mod = _adaln(c_all, w_ada[l], b_ada[l])
    mod_p = [mod[:batch, None, k * d:(k + 1) * d] for k in range(6)]
    mod_s = [jnp.repeat(mod[batch:batch + n_seq, k * d:(k + 1) * d], t_new, axis=0) for k in range(6)]

    w_in_b = w_in[l].astype(BF16)
    w_out_b = w_out[l].astype(BF16)
    r_hi, r_lo = _split_bf16(w_router[l])
    sh_w = (w_gate_sh[l].astype(BF16), w_up_sh[l].astype(BF16), w_down_sh[l].astype(BF16))
    diff_scale = DIFF_DIM ** -0.5

    groups = ((xp.reshape(n_p, d), mod_p, seq, jnp.arange(seq, dtype=jnp.int32)),
              (xs.reshape(n_s, d), mod_s, 1, past + jnp.arange(n_s, dtype=jnp.int32) % t_new))
    proj_out = []
    for x2d, md, rpg, pos in groups:
        m = x2d.shape[0]
        tm = _tile(m, 512)
        h = _norm_mod(x2d, norm1_g[l], md[0], md[1], rpg, tm)
        tables = _rope_tables(pos)
        q_dt = [BF16] if rpg > 1 else [F32]
        (q,) = _proj(h, w_in_b, 0, d, tables, q_dt, True, diff_scale, tm)
        k32, k16 = _proj(h, w_in_b, d, d, tables, [F32, BF16], True, 1.0, tm)
        v32, v16 = _proj(h, w_in_b, 2 * d, d, tables, [F32, BF16], False, 1.0, tm)
        proj_out.append((q, k32, k16, v32, v16))

    (qp, kp32, kp16, vp32, vp16), (qs, ks32, _, vs32, _) = proj_out
    lam1 = lam.reshape(1)
    o_diff, o_sb = _prompt_attention(qp, kp16, vp16, lam1, subln_g[l], batch, seq, n_diff, lam_init,
                                     _tile(seq, 256))
    o_s = _sample_attention(l, qs.reshape(n_seq, t_new, d), ks32.reshape(n_seq, t_new, d),
                            vs32.reshape(n_seq, t_new, d), cache_k, cache_v, page_table, lam1, subln_g[l],
                            n_diff, lam_init).reshape(n_s, d)

    x1s, h2s, lgs = [], [], []
    for (x2d, md, rpg, _), (oa, ob, blk_b) in zip(groups, ((o_diff, o_sb, 0), (o_s, o_s, 1))):
        tm = _tile(x2d.shape[0], 256)
        x1, h2, lg = _outproj(oa, ob, blk_b, w_out_b, x2d, md[2], norm2_g[l], md[3], md[4], r_hi, r_lo, rpg, tm)
        x1s.append(x1)
        h2s.append(h2)
        lgs.append(lg)

    h2_all = jnp.concatenate(h2s, axis=0)
    lg_all = jnp.concatenate(lgs, axis=0)
    n_all = n_p + n_s
    t_route = _tile(n_all, 256, 128)
    idx_t, gate_t, rank_t, counts = _route(lg_all, router_bias[l], t_route)
    n_e = w_router.shape[-1]
    pad_start, n_rows, plan = _block_plan(counts[:, 0].astype(jnp.int32), TOP_K * n_all)
    dest = _dest(idx_t, rank_t, pad_start.astype(F32)[:, None], t_route)
    tok = jnp.broadcast_to(jnp.arange(n_all, dtype=jnp.int32), (TOP_K, n_all))
    row_tok = jnp.zeros((n_rows,), jnp.int32).at[dest.reshape(-1)].set(tok.reshape(-1))
    be, ob, first, nxt, slot, n_valid = plan
    plan = (be + l * n_e, ob, first, jnp.where(nxt >= 0, nxt + l * n_e, -1), slot, n_valid)
    flat = lambda w: w.reshape((-1,) + w.shape[2:])
    y_rows = _experts(h2_all[row_tok], plan, flat(w_gate_e), flat(w_up_e), flat(w_down_e))
    y_slots = y_rows[dest]
    gates = gate_t.T

    outs = []
    row0 = 0
    for (x2d, md, rpg, _), x1, h2 in zip(groups, x1s, h2s):
        m = x2d.shape[0]
        outs.append(_final(h2, *sh_w, y_slots, gates, row0, x1, md[5], final_g, rpg, _tile(m, 128)))
        row0 += m
    return outs[0], outs[1], kp32, vp32, ks32, vs32


def kernel(x_prompt, x_sample, cache_k, cache_v, page_table, c_prompt, c_sample, w_ada, b_ada, norm1_g, w_in, lambda_q1, lambda_k1, lambda_q2, lambda_k2, subln_g, w_out, norm2_g, w_router, router_bias, w_gate_e, w_up_e, w_down_e, w_gate_sh, w_up_sh, w_down_sh, final_g):
    depth = w_in.shape[0]
    assert depth == 1, "the final norm is fused into the last layer; only one layer is supported"
    batch, seq, d = x_prompt.shape
    n_seq, t_new, _ = x_sample.shape
    n_heads = d // HEAD_DIM
    n_c = batch + n_seq
    c_all = jnp.concatenate([c_prompt, c_sample, jnp.zeros((-n_c % 8, d), F32)], axis=0)
    l = 0
    lam_init = 0.8 - 0.6 * math.exp(-0.3 * l)
    lam = (jnp.exp(jnp.sum(lambda_q1[l] * lambda_k1[l])) - jnp.exp(jnp.sum(lambda_q2[l] * lambda_k2[l]))
           + lam_init).astype(F32)
    yp, ys, kp, vp, ks, vs = _layer(
        l, x_prompt, x_sample, cache_k, cache_v, page_table, c_all, w_ada, b_ada, norm1_g, w_in, lam,
        lam_init, subln_g, w_out, norm2_g, w_router, router_bias, w_gate_e, w_up_e, w_down_e,
        w_gate_sh, w_up_sh, w_down_sh, final_g)
    return (yp.reshape(batch, seq, d), ys.reshape(n_seq, t_new, d),
            kp.reshape(1, batch, seq, n_heads, HEAD_DIM), vp.reshape(1, batch, seq, n_heads, HEAD_DIM),
            ks.reshape(1, n_seq, t_new, n_heads, HEAD_DIM), vs.reshape(1, n_seq, t_new, n_heads, HEAD_DIM))
```

```python
import functools
import math

import jax
import jax.numpy as jnp
from jax import lax
from jax.experimental import pallas as pl
from jax.experimental.pallas import tpu as pltpu

F32 = jnp.float32
BF16 = jnp.bfloat16

HEAD_DIM = 128
DIFF_DIM = HEAD_DIM // 2
ROT_DIM = DIFF_DIM // 4
ROT_HALF = ROT_DIM // 2
ROPE_THETA = 500000.0
N_EXPERT_GROUPS = 8
TOPK_GROUPS = 4
TOP_K = 8
ROUTED_SCALE = 2.5
EPS = 1e-6
NEG = -1e30
EXPERT_ROWS = 128
VMEM_LIMIT = 56 * 1024 * 1024


def _params(*sem):
    return pltpu.CompilerParams(dimension_semantics=sem, vmem_limit_bytes=VMEM_LIMIT)


def _mod(ref):
    return ref[0] if len(ref.shape) == 3 else ref[...]


def _rms(x):
    return x * lax.rsqrt(jnp.mean(x * x, axis=-1, keepdims=True) + EPS)


def _adaln_kernel(c_ref, w_ref, b_ref, o_ref):
    c = c_ref[...]
    s = (c * jax.nn.sigmoid(c)).astype(BF16)
    o_ref[...] = jnp.dot(s, w_ref[...].astype(BF16), preferred_element_type=F32) + b_ref[...]


def _adaln(c, w, b):
    m, d = c.shape
    n = w.shape[1]
    tn = _tile(n, 1024, HEAD_DIM)
    return pl.pallas_call(
        _adaln_kernel,
        grid=(n // tn,),
        in_specs=[pl.BlockSpec((m, d), lambda j: (0, 0)),
                  pl.BlockSpec((d, tn), lambda j: (0, j)),
                  pl.BlockSpec((1, tn), lambda j: (0, j))],
        out_specs=pl.BlockSpec((m, tn), lambda j: (0, j)),
        out_shape=jax.ShapeDtypeStruct((m, n), F32),
        compiler_params=_params("arbitrary"),
        name="adaln",
    )(c, w, b.reshape(1, n))


def _norm_mod_kernel(x_ref, g_ref, sh_ref, sc_ref, h_ref):
    y = _rms(x_ref[...]) * g_ref[...]
    h_ref[...] = (y * (1.0 + _mod(sc_ref)) + _mod(sh_ref)).astype(h_ref.dtype)


def _mod_spec(arr, tm, rows_per_group, d):
    if arr.ndim == 3:
        return pl.BlockSpec((1, 1, d), lambda i: (i * tm // rows_per_group, 0, 0))
    return pl.BlockSpec((tm, d), lambda i: (i, 0))


def _norm_mod(x, g, shift, scale, rows_per_group, tm):
    m, d = x.shape
    return pl.pallas_call(
        _norm_mod_kernel,
        grid=(m // tm,),
        in_specs=[pl.BlockSpec((tm, d), lambda i: (i, 0)),
                  pl.BlockSpec((1, d), lambda i: (0, 0)),
                  _mod_spec(shift, tm, rows_per_group, d),
                  _mod_spec(scale, tm, rows_per_group, d)],
        out_specs=pl.BlockSpec((tm, d), lambda i: (i, 0)),
        out_shape=jax.ShapeDtypeStruct((m, d), BF16),
        compiler_params=_params("arbitrary"),
        name="norm_mod",
    )(x, g.reshape(1, d), shift, scale)


def _rope(x, cos, sa, sb):
    return x * cos + pltpu.roll(x, HEAD_DIM - ROT_HALF, 1) * sa + pltpu.roll(x, ROT_HALF, 1) * sb


def _proj_kernel(h_ref, w_ref, cos_ref, sa_ref, sb_ref, *o_refs, rope, diff_scale):
    acc = jnp.dot(h_ref[...], w_ref[...], preferred_element_type=F32)
    j = pl.program_id(0)

    def store(val):
        for o in o_refs:
            o[...] = val.astype(o.dtype)

    @pl.when(j == 0)
    def _():
        if rope:
            cos, sa, sb = cos_ref[...], sa_ref[...], sb_ref[...]
            blocks = [_rope(acc[:, c:c + HEAD_DIM], cos, sa, sb) * diff_scale
                      for c in range(0, acc.shape[1], HEAD_DIM)]
            store(jnp.concatenate(blocks, axis=1))
        else:
            store(acc)

    @pl.when(j != 0)
    def _():
        store(acc)


def _proj(h, w, col0, n, tables, out_dtypes, rope, diff_scale, tm):
    m, d = h.shape
    tn = n // 2
    cos, sa, sb = tables
    n_tab = cos.shape[0] // tm
    tab_spec = pl.BlockSpec((tm, HEAD_DIM), lambda j, i: (i % n_tab, 0))
    off = col0 // tn
    return pl.pallas_call(
        functools.partial(_proj_kernel, rope=rope, diff_scale=diff_scale),
        grid=(2, m // tm),
        in_specs=[pl.BlockSpec((tm, d), lambda j, i: (i, 0)),
                  pl.BlockSpec((d, tn), lambda j, i: (0, off + j)),
                  tab_spec, tab_spec, tab_spec],
        out_specs=[pl.BlockSpec((tm, tn), lambda j, i: (i, j)) for _ in out_dtypes],
        out_shape=[jax.ShapeDtypeStruct((m, n), dt) for dt in out_dtypes],
        compiler_params=_params("arbitrary", "arbitrary"),
        name="proj",
    )(h, w, cos, sa, sb)


def _rope_tables(pos):
    inv_freq = ROPE_THETA ** (-jnp.arange(ROT_HALF, dtype=F32) / ROT_HALF)
    ang = pos.astype(F32)[:, None] * inv_freq[None, :]
    cos, sin = jnp.cos(ang), jnp.sin(ang)
    t = pos.shape[0]
    one = jnp.ones((t, DIFF_DIM - ROT_DIM), F32)
    zero = jnp.zeros((t, DIFF_DIM - ROT_DIM), F32)
    z8 = jnp.zeros((t, ROT_HALF), F32)
    cos_l = jnp.concatenate([cos, cos, one], axis=1)
    sa_l = jnp.concatenate([-sin, z8, zero], axis=1)
    sb_l = jnp.concatenate([z8, sin, zero], axis=1)
    return tuple(jnp.concatenate([a, a], axis=1) for a in (cos_l, sa_l, sb_l))


def _split_maps(q):
    lane = lax.broadcasted_iota(jnp.int32, q.shape, 1)
    zero = jnp.zeros_like(q)
    return jnp.concatenate([jnp.where(lane < DIFF_DIM, q, zero),
                            jnp.where(lane >= DIFF_DIM, q, zero)], axis=0)


def _diff_attn_kernel(lam_ref, q_ref, k_ref, v_ref, g_ref, o_ref, m_sc, l_sc, acc_sc, *, t, out_scale):
    i = pl.program_id(2)
    q2 = _split_maps(q_ref[...])
    m_sc[...] = jnp.full(m_sc.shape, NEG, F32)
    l_sc[...] = jnp.zeros(l_sc.shape, F32)
    acc_sc[...] = jnp.zeros(acc_sc.shape, F32)

    def step(j, masked):
        start = pl.multiple_of(j * t, t)
        kj = k_ref[pl.ds(start, t), :]
        vj = v_ref[pl.ds(start, t), :]
        s = lax.dot_general(q2, kj, (((1,), (1,)), ((), ())), preferred_element_type=F32)
        if masked:
            row = lax.broadcasted_iota(jnp.int32, (2 * t, t), 0)
            row = jnp.where(row >= t, row - t, row)
            col = lax.broadcasted_iota(jnp.int32, (2 * t, t), 1)
            s = jnp.where(col <= row, s, NEG)
        m_old = m_sc[...]
        m_new = jnp.maximum(m_old, jnp.max(s, axis=-1, keepdims=True))
        alpha = jnp.exp(m_old - m_new)
        p = jnp.exp(s - m_new)
        l_sc[...] = alpha * l_sc[...] + jnp.sum(p, axis=-1, keepdims=True)
        acc_sc[...] = alpha * acc_sc[...] + jnp.dot(p.astype(BF16), vj, preferred_element_type=F32)
        m_sc[...] = m_new

    def body(j, c):
        step(j, False)
        return c

    lax.fori_loop(0, i, body, 0)
    step(i, True)

    o2 = acc_sc[...] / l_sc[...]
    o = o2[:t] - lam_ref[0] * o2[t:]
    o_ref[...] = (_rms(o) * g_ref[...] * out_scale).astype(o_ref.dtype)


def _sb_weights(z, strict):
    soft = jnp.log(1.0 + jnp.exp(-jnp.abs(z)))
    log_beta = jnp.minimum(z, 0.0) - soft
    log_1m = jnp.minimum(-z, 0.0) - soft
    if strict is not None:
        log_1m = jnp.where(strict, log_1m, 0.0)
    return log_beta, log_1m


def _later_sum(x, upper):
    hi = x.astype(BF16)
    lo = (x - hi.astype(F32)).astype(BF16)
    return (jnp.dot(hi, upper, preferred_element_type=F32)
            + jnp.dot(lo, upper, preferred_element_type=F32))


def _upper(t):
    r = lax.broadcasted_iota(jnp.int32, (t, t), 0)
    c = lax.broadcasted_iota(jnp.int32, (t, t), 1)
    return jnp.where(r > c, 1.0, 0.0).astype(BF16)


def _sb_attn_kernel(q_ref, k_ref, v_ref, o_ref, c_sc, acc_sc, *, t, scale):
    i = pl.program_id(2)
    q = q_ref[...]
    upper = _upper(t)
    c_sc[...] = jnp.zeros(c_sc.shape, F32)
    acc_sc[...] = jnp.zeros(acc_sc.shape, F32)

    def step(j, masked):
        start = pl.multiple_of(j * t, t)
        kj = k_ref[pl.ds(start, t), :]
        vj = v_ref[pl.ds(start, t), :]
        z = lax.dot_general(q, kj, (((1,), (1,)), ((), ())), preferred_element_type=F32) * scale
        strict = None
        if masked:
            row = lax.broadcasted_iota(jnp.int32, (t, t), 0)
            col = lax.broadcasted_iota(jnp.int32, (t, t), 1)
            strict = col < row
        log_beta, log_1m = _sb_weights(z, strict)
        a = jnp.exp(log_beta + _later_sum(log_1m, upper) + c_sc[...])
        if masked:
            a = jnp.where(strict, a, 0.0)
        acc_sc[...] += jnp.dot(a.astype(BF16), vj, preferred_element_type=F32)
        c_sc[...] += jnp.sum(log_1m, axis=-1, keepdims=True)

    step(i, True)

    def body(r, c):
        step(i - 1 - r, False)
        return c

    lax.fori_loop(0, i, body, 0)
    o_ref[...] = acc_sc[...].astype(o_ref.dtype)


def _prompt_attention(q, k, v, lam, subln_g, batch, seq, n_diff, lam_init, t):
    m, d = q.shape
    n_heads = d // HEAD_DIM
    n_sb = n_heads - n_diff
    nq = seq // t
    params = _params("arbitrary", "arbitrary", "arbitrary")

    def q_spec(h0):
        return pl.BlockSpec((t, HEAD_DIM), lambda b, h, i: (b * nq + i, h0 + h))

    def kv_spec(h0):
        return pl.BlockSpec((seq, HEAD_DIM), lambda b, h, i: (b, h0 + h))

    o_spec = pl.BlockSpec((t, HEAD_DIM), lambda b, h, i: (b * nq + i, h))
    o_diff = pl.pallas_call(
        functools.partial(_diff_attn_kernel, t=t, out_scale=1.0 - lam_init),
        grid=(batch, n_diff, nq),
        in_specs=[pl.BlockSpec(memory_space=pltpu.SMEM), q_spec(0), kv_spec(0), kv_spec(0),
                  pl.BlockSpec((1, HEAD_DIM), lambda b, h, i: (0, 0))],
        out_specs=o_spec,
        out_shape=jax.ShapeDtypeStruct((m, n_diff * HEAD_DIM), BF16),
        scratch_shapes=[pltpu.VMEM((2 * t, 1), F32), pltpu.VMEM((2 * t, 1), F32),
                        pltpu.VMEM((2 * t, HEAD_DIM), F32)],
        compiler_params=params,
        name="diff_attn",
    )(lam, q, k, v, subln_g.reshape(1, HEAD_DIM))
    o_sb = pl.pallas_call(
        functools.partial(_sb_attn_kernel, t=t, scale=HEAD_DIM ** -0.5),
        grid=(batch, n_sb, nq),
        in_specs=[q_spec(n_diff), kv_spec(n_diff), kv_spec(n_diff)],
        out_specs=o_spec,
        out_shape=jax.ShapeDtypeStruct((m, n_sb * HEAD_DIM), BF16),
        scratch_shapes=[pltpu.VMEM((t, 1), F32), pltpu.VMEM((t, HEAD_DIM), F32)],
        compiler_params=params,
        name="sb_attn",
    )(q, k, v)
    return o_diff, o_sb


def _sample_attn_kernel(pt_ref, lam_ref, q_ref, kn_ref, vn_ref, *rest, n_heads, n_diff, t_new, n_par,
                        last_step, sb_scale, out_scale):
    k_refs, v_refs = rest[:n_par], rest[n_par:2 * n_par]
    g_ref, o_ref, qm_sc, m_sc, l_sc, acc_sc = rest[2 * n_par:]
    p = pl.program_id(1)
    half = 4
    rows = 2 * half
    row = lax.broadcasted_iota(jnp.int32, (rows, 1), 0)
    t_row = row % half
    page = k_refs[0].shape[0] // n_heads
    upper = _upper(page)

    def head_rows(ref, h):
        return ref[pl.ds(h, page, stride=n_heads), :].astype(BF16)
    nd8 = rows * n_diff
    ns8 = rows * (n_heads - n_diff)

    def cols(h):
        return slice(h * HEAD_DIM, (h + 1) * HEAD_DIM)

    def hrows(h):
        return slice(h * rows, (h + 1) * rows)

    @pl.when(p == 0)
    def _init():
        pad = jnp.zeros((half - t_new, HEAD_DIM), F32)
        for h in range(n_heads):
            qh = jnp.concatenate([q_ref[0, :, cols(h)], pad], axis=0) if t_new < half else q_ref[0, :, cols(h)]
            if h < n_diff:
                lane = lax.broadcasted_iota(jnp.int32, qh.shape, 1)
                qm = jnp.concatenate([jnp.where(lane < DIFF_DIM, qh, 0.0),
                                      jnp.where(lane >= DIFF_DIM, qh, 0.0)], axis=0)
            else:
                qm = jnp.concatenate([qh, jnp.zeros_like(qh)], axis=0)
            qm_sc[h] = qm
            kn = kn_ref[0, :, cols(h)]
            vn = vn_ref[0, :, cols(h)]
            s = [jnp.sum(qm * kn[j:j + 1, :], axis=-1, keepdims=True) for j in range(t_new)]
            if h < n_diff:
                ok = [t_row >= j for j in range(t_new)]
                m = functools.reduce(jnp.maximum, [jnp.where(ok[j], s[j], NEG) for j in range(t_new)])
                pj = [jnp.where(ok[j], jnp.exp(s[j] - m), 0.0) for j in range(t_new)]
                m_sc[hrows(h)] = m
                l_sc[hrows(h)] = sum(pj)
                acc_sc[hrows(h)] = sum(pj[j] * vn[j:j + 1, :] for j in range(t_new))
            else:
                strict = [t_row > j for j in range(t_new)]
                lb, l1 = zip(*[_sb_weights(s[j] * sb_scale, strict[j]) for j in range(t_new)])
                acc = jnp.zeros((rows, HEAD_DIM), F32)
                later = jnp.zeros((rows, 1), F32)
                for j in reversed(range(t_new)):
                    acc = acc + jnp.where(strict[j], jnp.exp(lb[j] + later), 0.0) * vn[j:j + 1, :]
                    later = later + l1[j]
                m_sc[hrows(h)] = later
                l_sc[hrows(h)] = jnp.ones((rows, 1), F32)
                acc_sc[hrows(h)] = acc

    nt = (((1,), (1,)), ((), ()))
    s = jnp.concatenate(
        [jnp.concatenate([lax.dot_general(qm_sc[h].astype(BF16), head_rows(k_refs[a], h), nt,
                                          preferred_element_type=F32) for h in range(n_heads)], axis=0)
         for a in range(n_par)], axis=1)

    m_old = m_sc[:nd8]
    m_new = jnp.maximum(m_old, jnp.max(s[:nd8], axis=-1, keepdims=True))
    alpha = jnp.exp(m_old - m_new)
    p_diff = jnp.exp(s[:nd8] - m_new)
    l_sc[:nd8] = alpha * l_sc[:nd8] + jnp.sum(p_diff, axis=-1, keepdims=True)
    m_sc[:nd8] = m_new

    log_beta, log_1m = _sb_weights(s[nd8:] * sb_scale, None)
    chunks = [log_1m[:, a * page:(a + 1) * page] for a in range(n_par)]
    within = _later_sum(jnp.concatenate(chunks, axis=0), upper)
    later = m_sc[nd8:]
    tails = [None] * n_par
    for a in reversed(range(n_par)):
        tails[a] = within[a * ns8:(a + 1) * ns8] + later
        later = later + jnp.sum(chunks[a], axis=-1, keepdims=True)
    m_sc[nd8:] = later
    p_sb = jnp.exp(log_beta + jnp.concatenate(tails, axis=1))

    pm = jnp.concatenate([p_diff, p_sb], axis=0)
    pv = []
    for h in range(n_heads):
        parts = [jnp.dot(pm[hrows(h), a * page:(a + 1) * page].astype(BF16),
                         head_rows(v_refs[a], h), preferred_element_type=F32)
                 for a in range(n_par)]
        pv.append(functools.reduce(lambda x, y: x + y, parts))
    decay = jnp.concatenate([alpha, jnp.ones((ns8, 1), F32)], axis=0)
    acc_sc[...] = decay * acc_sc[...] + jnp.concatenate(pv, axis=0)

    @pl.when(p == last_step)
    def _fin():
        for h in range(n_heads):
            if h < n_diff:
                o2 = acc_sc[hrows(h)] / l_sc[hrows(h)]
                o = o2[:half] - lam_ref[0] * o2[half:]
                o = _rms(o) * g_ref[...] * out_scale
            else:
                o = acc_sc[hrows(h)][:half]
            o_ref[0, :, cols(h)] = o[:t_new]


def _sample_attention(l, q, k_new, v_new, cache_k, cache_v, page_table, lam, subln_g, n_diff, lam_init):
    n_seq, t_new, d = q.shape
    n_heads = d // HEAD_DIM
    n_pages = page_table.shape[1]
    n_phys, page = cache_k.shape[1:3]
    assert t_new <= 4
    n_par = _tile(n_pages, 4, 1)
    n_steps = n_pages // n_par
    new_spec = pl.BlockSpec((1, t_new, d), lambda b, p, pt: (b, 0, 0))
    cache_k = cache_k.reshape(-1, HEAD_DIM)
    cache_v = cache_v.reshape(-1, HEAD_DIM)

    def cache_spec(a):
        return pl.BlockSpec((page * n_heads, HEAD_DIM),
                            lambda b, p, pt: (l * n_phys + pt[b, n_pages - (p + 1) * n_par + a], 0))

    cache_specs = [cache_spec(a) for a in range(n_par)]
    rows = 8 * n_heads
    return pl.pallas_call(
        functools.partial(_sample_attn_kernel, n_heads=n_heads, n_diff=n_diff, t_new=t_new, n_par=n_par,
                          last_step=n_steps - 1, sb_scale=HEAD_DIM ** -0.5, out_scale=1.0 - lam_init),
        grid_spec=pltpu.PrefetchScalarGridSpec(
            num_scalar_prefetch=1,
            grid=(n_seq, n_steps),
            in_specs=[pl.BlockSpec(memory_space=pltpu.SMEM), new_spec, new_spec, new_spec,
                      *cache_specs, *cache_specs,
                      pl.BlockSpec((1, HEAD_DIM), lambda b, p, pt: (0, 0))],
            out_specs=new_spec,
            scratch_shapes=[pltpu.VMEM((n_heads, 8, HEAD_DIM), F32), pltpu.VMEM((rows, 1), F32),
                            pltpu.VMEM((rows, 1), F32), pltpu.VMEM((rows, HEAD_DIM), F32)]),
        out_shape=jax.ShapeDtypeStruct((n_seq, t_new, d), F32),
        compiler_params=_params("arbitrary", "arbitrary"),
        name="sample_attn",
    )(page_table, lam, q, k_new, v_new, *([cache_k] * n_par), *([cache_v] * n_par),
      subln_g.reshape(1, HEAD_DIM))


def _split_bf16(x):
    hi = x.astype(BF16)
    return hi, (x - hi.astype(F32)).astype(BF16)


def _outproj_kernel(oa_ref, ob_ref, wa_ref, wb_ref, x_ref, g1_ref, n2_ref, sh_ref, sc_ref,
                    rhi_ref, rlo_ref, x1_ref, h2_ref, lg_ref):
    attn = (jnp.dot(oa_ref[...].astype(BF16), wa_ref[...], preferred_element_type=F32)
            + jnp.dot(ob_ref[...].astype(BF16), wb_ref[...], preferred_element_type=F32))
    x1 = x_ref[...] + _mod(g1_ref) * attn
    x1_ref[...] = x1
    h2 = _rms(x1) * n2_ref[...] * (1.0 + _mod(sc_ref)) + _mod(sh_ref)
    hi, lo = _split_bf16(h2)
    h2_ref[...] = hi.astype(F32)
    lg_ref[...] = (jnp.dot(hi, rhi_ref[...], preferred_element_type=F32)
                   + jnp.dot(lo, rhi_ref[...], preferred_element_type=F32)
                   + jnp.dot(hi, rlo_ref[...], preferred_element_type=F32))


def _outproj(o_a, o_b, blk_b, w_out, x, g1, norm2_g, sh2, sc2, r_hi, r_lo, rows_per_group, tm):
    m, d = x.shape
    dh = d // 2
    n_e = r_hi.shape[1]
    row = lambda i: (i, 0)
    const = lambda i: (0, 0)
    return pl.pallas_call(
        _outproj_kernel,
        grid=(m // tm,),
        in_specs=[pl.BlockSpec((tm, dh), row),
                  pl.BlockSpec((tm, dh), lambda i: (i, blk_b)),
                  pl.BlockSpec((dh, d), const),
                  pl.BlockSpec((dh, d), lambda i: (1, 0)),
                  pl.BlockSpec((tm, d), row),
                  _mod_spec(g1, tm, rows_per_group, d),
                  pl.BlockSpec((1, d), const),
                  _mod_spec(sh2, tm, rows_per_group, d),
                  _mod_spec(sc2, tm, rows_per_group, d),
                  pl.BlockSpec((d, n_e), const),
                  pl.BlockSpec((d, n_e), const)],
        out_specs=[pl.BlockSpec((tm, d), row), pl.BlockSpec((tm, d), row), pl.BlockSpec((tm, n_e), row)],
        out_shape=[jax.ShapeDtypeStruct((m, d), F32), jax.ShapeDtypeStruct((m, d), F32),
                   jax.ShapeDtypeStruct((m, n_e), F32)],
        compiler_params=_params("arbitrary"),
        name="outproj",
    )(o_a, o_b, w_out, w_out, x, g1, norm2_g.reshape(1, d), sh2, sc2, r_hi, r_lo)


def _route_kernel(lg_ref, bias_ref, idx_ref, gate_ref, rank_ref, cnt_ref, cnt_sc):
    @pl.when(pl.program_id(0) == 0)
    def _():
        cnt_sc[...] = jnp.zeros(cnt_sc.shape, F32)

    scores = jax.nn.sigmoid(lg_ref[...]).T
    n_e, tm = scores.shape
    per = n_e // N_EXPERT_GROUPS
    choice = scores + bias_ref[...]
    ninf = -jnp.inf
    sub = lax.broadcasted_iota(jnp.int32, (per, tm), 0).astype(F32)
    grp = []
    for g in range(N_EXPERT_GROUPS):
        cg = choice[g * per:(g + 1) * per]
        m1 = jnp.max(cg, axis=0, keepdims=True)
        i1 = jnp.min(jnp.where(cg == m1, sub, float(per)), axis=0, keepdims=True)
        m2 = jnp.max(jnp.where(sub == i1, ninf, cg), axis=0, keepdims=True)
        grp.append(m1 + m2)
    masked = []
    for g in range(N_EXPERT_GROUPS):
        beaten = sum(((grp[o] > grp[g]) if o > g else (grp[o] >= grp[g])).astype(F32)
                     for o in range(N_EXPERT_GROUPS) if o != g)
        masked.append(jnp.where(beaten < TOPK_GROUPS, choice[g * per:(g + 1) * per], NEG))
    cm = jnp.concatenate(masked, axis=0)
    eid = lax.broadcasted_iota(jnp.int32, (n_e, tm), 0).astype(F32)
    r = lax.broadcasted_iota(jnp.int32, (tm, tm), 0)
    c = lax.broadcasted_iota(jnp.int32, (tm, tm), 1)
    earlier = jnp.where(r < c, 1.0, 0.0).astype(BF16)
    count = cnt_sc[...]
    idx, gate, rank = [], [], []
    for _ in range(TOP_K):
        mx = jnp.max(cm, axis=0, keepdims=True)
        ik = jnp.min(jnp.where(cm == mx, eid, float(n_e)), axis=0, keepdims=True)
        hit = eid == ik
        gate.append(jnp.sum(jnp.where(hit, scores, 0.0), axis=0, keepdims=True))
        cm = jnp.where(hit, ninf, cm)
        idx.append(ik)
        one = jnp.where(hit, 1.0, 0.0)
        before = jnp.dot(one.astype(BF16), earlier, preferred_element_type=F32)
        rank.append(jnp.sum(jnp.where(hit, before + count, 0.0), axis=0, keepdims=True))
        count = count + jnp.sum(one, axis=1, keepdims=True)
    cnt_sc[...] = count
    cnt_ref[...] = count
    gates = jnp.concatenate(gate, axis=0)
    idx_ref[...] = jnp.concatenate(idx, axis=0).astype(jnp.int32)
    rank_ref[...] = jnp.concatenate(rank, axis=0).astype(jnp.int32)
    gate_ref[...] = gates / jnp.sum(gates, axis=0, keepdims=True) * ROUTED_SCALE


def _route(logits, bias, tm):
    m, n_e = logits.shape
    slot = pl.BlockSpec((TOP_K, tm), lambda i: (0, i))
    return pl.pallas_call(
        _route_kernel,
        grid=(m // tm,),
        in_specs=[pl.BlockSpec((tm, n_e), lambda i: (i, 0)),
                  pl.BlockSpec((n_e, 1), lambda i: (0, 0))],
        out_specs=[slot, slot, slot, pl.BlockSpec((n_e, 1), lambda i: (0, 0))],
        out_shape=[jax.ShapeDtypeStruct((TOP_K, m), jnp.int32), jax.ShapeDtypeStruct((TOP_K, m), F32),
                   jax.ShapeDtypeStruct((TOP_K, m), jnp.int32), jax.ShapeDtypeStruct((n_e, 1), F32)],
        scratch_shapes=[pltpu.VMEM((n_e, 1), F32)],
        compiler_params=_params("arbitrary"),
        name="route",
    )(logits, bias.reshape(n_e, 1))


def _dest_kernel(idx_ref, rank_ref, start_ref, dest_ref):
    idx = idx_ref[...]
    k, tm = idx.shape
    n_e = start_ref.shape[0]
    eid = lax.broadcasted_iota(jnp.int32, (n_e, tm), 0)
    base = [jnp.sum(jnp.where(eid == idx[j:j + 1, :], start_ref[...], 0.0), axis=0, keepdims=True)
            for j in range(k)]
    dest_ref[...] = jnp.concatenate(base, axis=0).astype(jnp.int32) + rank_ref[...]


def _dest(idx_t, rank_t, start, tm):
    k, m = idx_t.shape
    n_e = start.shape[0]
    slot = pl.BlockSpec((k, tm), lambda i: (0, i))
    return pl.pallas_call(
        _dest_kernel,
        grid=(m // tm,),
        in_specs=[slot, slot, pl.BlockSpec((n_e, 1), lambda i: (0, 0))],
        out_specs=slot,
        out_shape=jax.ShapeDtypeStruct((k, m), jnp.int32),
        compiler_params=_params("arbitrary"),
        name="dest",
    )(idx_t, rank_t, start)


def _expert_kernel(be_ref, ob_ref, first_ref, next_ref, slot_ref, nv_ref, x_ref, wg_hbm, wu_hbm, wd_hbm,
                   y_ref, wg_buf, wu_buf, wd_buf, wg_sc, wu_sc, wd_sc, sem):
    i = pl.program_id(0)

    def copies(e, slot):
        return (pltpu.make_async_copy(wg_hbm.at[e], wg_buf.at[slot], sem.at[slot, 0]),
                pltpu.make_async_copy(wu_hbm.at[e], wu_buf.at[slot], sem.at[slot, 1]),
                pltpu.make_async_copy(wd_hbm.at[e], wd_buf.at[slot], sem.at[slot, 2]))

    @pl.when(first_ref[i] == 1)
    def _():
        slot = slot_ref[i]

        @pl.when(i == 0)
        def _():
            for cp in copies(be_ref[i], slot):
                cp.start()

        @pl.when(next_ref[i] >= 0)
        def _():
            for cp in copies(next_ref[i], 1 - slot):
                cp.start()

        for cp in copies(be_ref[i], slot):
            cp.wait()

        wg_sc[...] = wg_buf[slot].astype(BF16)
        wu_sc[...] = wu_buf[slot].astype(BF16)
        wd_sc[...] = wd_buf[slot].astype(BF16)

    @pl.when(i < nv_ref[0])
    def _():
        x = x_ref[...].astype(BF16)
        a = jnp.dot(x, wg_sc[...], preferred_element_type=F32)
        u = jnp.dot(x, wu_sc[...], preferred_element_type=F32)
        mid = (a * jax.nn.sigmoid(a) * u).astype(BF16)
        y_ref[...] = jnp.dot(mid, wd_sc[...], preferred_element_type=F32)

    @pl.when(i >= nv_ref[0])
    def _():
        y_ref[...] = jnp.zeros(y_ref.shape, F32)


def _experts(x_rows, plan, w_gate, w_up, w_down):
    r, d = x_rows.shape
    n_e, _, f = w_gate.shape
    tm = EXPERT_ROWS
    n_blocks = r // tm
    rows = lambda i, be, ob, *_: (ob[i], 0)
    hbm = pl.BlockSpec(memory_space=pl.ANY)
    return pl.pallas_call(
        _expert_kernel,
        grid_spec=pltpu.PrefetchScalarGridSpec(
            num_scalar_prefetch=6,
            grid=(n_blocks,),
            in_specs=[pl.BlockSpec((tm, d), rows), hbm, hbm, hbm],
            out_specs=pl.BlockSpec((tm, d), lambda i, *_: (i, 0)),
            scratch_shapes=[pltpu.VMEM((2, d, f), F32), pltpu.VMEM((2, d, f), F32), pltpu.VMEM((2, f, d), F32),
                            pltpu.VMEM((d, f), BF16), pltpu.VMEM((d, f), BF16), pltpu.VMEM((f, d), BF16),
                            pltpu.SemaphoreType.DMA((2, 3))]),
        out_shape=jax.ShapeDtypeStruct((r, d), F32),
        compiler_params=_params("arbitrary"),
        name="experts",
    )(*plan, x_rows, w_gate, w_up, w_down)


def _block_plan(counts, n_slots):
    n_e = counts.shape[0]
    tm = EXPERT_ROWS
    padded = (counts + tm - 1) // tm * tm
    pad_end = jnp.cumsum(padded)
    pad_start = pad_end - padded
    n_blocks = -(-n_slots // tm) + n_e
    blk = jnp.arange(n_blocks, dtype=jnp.int32)
    block_expert = jnp.sum((pad_end[None, :] <= blk[:, None] * tm).astype(jnp.int32), axis=1)
    block_expert = jnp.minimum(block_expert, n_e - 1)
    n_valid = (pad_end[-1] // tm).astype(jnp.int32)
    out_block = jnp.minimum(blk, n_valid)
    prev = jnp.concatenate([jnp.full((1,), -1, jnp.int32), block_expert[:-1]])
    first = ((blk < n_valid) & (block_expert != prev)).astype(jnp.int32)
    used = counts > 0
    ids = jnp.arange(n_e, dtype=jnp.int32)
    later = lax.cummin(jnp.where(used, ids, n_e), axis=0, reverse=True)
    nxt = jnp.concatenate([later[1:], jnp.full((1,), n_e, jnp.int32)])
    nxt = jnp.where(nxt < n_e, nxt, -1)
    ordinal = jnp.cumsum(used.astype(jnp.int32)) - 1
    plan = (block_expert, out_block, first, nxt[block_expert], ordinal[block_expert] % 2, n_valid.reshape(1))
    return pad_start, n_blocks * tm, plan


def _final_kernel(h_ref, wg_ref, wu_ref, wd_ref, yk_ref, gt_ref, x1_ref, g2_ref, fg_ref, y_ref):
    h = h_ref[...].astype(BF16)
    a = jnp.dot(h, wg_ref[...], preferred_element_type=F32)
    u = jnp.dot(h, wu_ref[...], preferred_element_type=F32)
    mid = (a * jax.nn.sigmoid(a) * u).astype(BF16)
    moe = jnp.dot(mid, wd_ref[...], preferred_element_type=F32)
    gt = gt_ref[...]
    for j in range(yk_ref.shape[0]):
        moe = moe + yk_ref[j] * gt[:, j:j + 1]
    x2 = x1_ref[...] + _mod(g2_ref) * moe
    y_ref[...] = _rms(x2) * fg_ref[...]


def _final(h2, wg, wu, wd, y_slots, gates, row0, x1, g2, final_g, rows_per_group, tm):
    m, d = x1.shape
    f = wg.shape[1]
    k = y_slots.shape[0]
    assert row0 % tm == 0
    b0 = row0 // tm
    row = lambda i: (i, 0)
    const = lambda i: (0, 0)
    return pl.pallas_call(
        _final_kernel,
        grid=(m // tm,),
        in_specs=[pl.BlockSpec((tm, d), row), pl.BlockSpec((d, f), const), pl.BlockSpec((d, f), const),
                  pl.BlockSpec((f, d), const), pl.BlockSpec((k, tm, d), lambda i: (0, b0 + i, 0)),
                  pl.BlockSpec((tm, k), lambda i: (b0 + i, 0)), pl.BlockSpec((tm, d), row),
                  _mod_spec(g2, tm, rows_per_group, d), pl.BlockSpec((1, d), const)],
        out_specs=pl.BlockSpec((tm, d), row),
        out_shape=jax.ShapeDtypeStruct((m, d), F32),
        compiler_params=_params("arbitrary"),
        name="final",
    )(h2, wg, wu, wd, y_slots, gates, x1, g2, final_g.reshape(1, d))


def _tile(m, cap, align=8):
    for t in range(min(m, cap) // align * align, 0, -align):
        if m % t == 0:
            return t
    raise ValueError(f"no tile for {m} rows")


def _layer(l, xp, xs, cache_k, cache_v, page_table, c_all, w_ada, b_ada, norm1_g, w_in, lam, lam_init,
           subln_g, w_out, norm2_g, w_router, router_bias, w_gate_e, w_up_e, w_down_e,
           w_gate_sh, w_up_sh, w_down_sh, final_g):
    batch, seq, d = xp.shape
    n_seq, t_new, _ = xs.shape
    n_heads = d // HEAD_DIM
    n_diff = n_heads // 2
    n_p, n_s = batch * seq, n_seq * t_new
    past = page_table.shape[1] * cache_k.shape[2]

    mod = _adaln(c_all, w_ada[l], b_ada[l])
    mod_p = [mod[:batch, None, k * d:(k + 1) * d] for k in range(6)]
    mod_s = [jnp.repeat(mod[batch:batch + n_seq, k * d:(k + 1) * d], t_new, axis=0) for k in range(6)]

    w_in_b = w_in[l].astype(BF16)
    w_out_b = w_out[l].astype(BF16)
    r_hi, r_lo = _split_bf16(w_router[l])
    sh_w = (w_gate_sh[l].astype(BF16), w_up_sh[l].astype(BF16), w_down_sh[l].astype(BF16))
    diff_scale = DIFF_DIM ** -0.5

    groups = ((xp.reshape(n_p, d), mod_p, seq, jnp.arange(seq, dtype=jnp.int32)),
              (xs.reshape(n_s, d), mod_s, 1, past + jnp.arange(n_s, dtype=jnp.int32) % t_new))
    proj_out = []
    for x2d, md, rpg, pos in groups:
        m = x2d.shape[0]
        tm = _tile(m, 512)
        h = _norm_mod(x2d, norm1_g[l], md[0], md[1], rpg, tm)
        tables = _rope_tables(pos)
        q_dt = [BF16] if rpg > 1 else [F32]
        (q,) = _proj(h, w_in_b, 0, d, tables, q_dt, True, diff_scale, tm)
        k32, k16 = _proj(h, w_in_b, d, d, tables, [F32, BF16], True, 1.0, tm)
        v32, v16 = _proj(h, w_in_b, 2 * d, d, tables, [F32, BF16], False, 1.0, tm)
        proj_out.append((q, k32, k16, v32, v16))

    (qp, kp32, kp16, vp32, vp16), (qs, ks32, _, vs32, _) = proj_out
    lam1 = lam.reshape(1)
    o_diff, o_sb = _prompt_attention(qp, kp16, vp16, lam1, subln_g[l], batch, seq, n_diff, lam_init,
                                     _tile(seq, 512))
    o_s = _sample_attention(l, qs.reshape(n_seq, t_new, d), ks32.reshape(n_seq, t_new, d),
                            vs32.reshape(n_seq, t_new, d), cache_k, cache_v, page_table, lam1, subln_g[l],
                            n_diff, lam_init).reshape(n_s, d)

    x1s, h2s, lgs = [], [], []
    for (x2d, md, rpg, _), (oa, ob, blk_b) in zip(groups, ((o_diff, o_sb, 0), (o_s, o_s, 1))):
        tm = _tile(x2d.shape[0], 256)
        x1, h2, lg = _outproj(oa, ob, blk_b, w_out_b, x2d, md[2], norm2_g[l], md[3], md[4], r_hi, r_lo, rpg, tm)
        x1s.append(x1)
        h2s.append(h2)
        lgs.append(lg)

    h2_all = jnp.concatenate(h2s, axis=0)
    lg_all = jnp.concatenate(lgs, axis=0)
    n_all = n_p + n_s
    t_route = _tile(n_all, 256, 128)
    idx_t, gate_t, rank_t, counts = _route(lg_all, router_bias[l], t_route)
    n_e = w_router.shape[-1]
    pad_start, n_rows, plan = _block_plan(counts[:, 0].astype(jnp.int32), TOP_K * n_all)
    dest = _dest(idx_t, rank_t, pad_start.astype(F32)[:, None], t_route)
    tok = jnp.broadcast_to(jnp.arange(n_all, dtype=jnp.int32), (TOP_K, n_all))
    row_tok = jnp.zeros((n_rows,), jnp.int32).at[dest.reshape(-1)].set(tok.reshape(-1))
    be, ob, first, nxt, slot, n_valid = plan
    plan = (be + l * n_e, ob, first, jnp.where(nxt >= 0, nxt + l * n_e, -1), slot, n_valid)
    flat = lambda w: w.reshape((-1,) + w.shape[2:])
    y_rows = _experts(h2_all[row_tok], plan, flat(w_gate_e), flat(w_up_e), flat(w_down_e))
    y_slots = y_rows[dest]
    gates = gate_t.T

    outs = []
    row0 = 0
    for (x2d, md, rpg, _), x1, h2 in zip(groups, x1s, h2s):
        m = x2d.shape[0]
        outs.append(_final(h2, *sh_w, y_slots, gates, row0, x1, md[5], final_g, rpg, _tile(m, 128)))
        row0 += m
    return outs[0], outs[1], kp32, vp32, ks32, vs32


def kernel(x_prompt, x_sample, cache_k, cache_v, page_table, c_prompt, c_sample, w_ada, b_ada, norm1_g, w_in, lambda_q1, lambda_k1, lambda_q2, lambda_k2, subln_g, w_out, norm2_g, w_router, router_bias, w_gate_e, w_up_e, w_down_e, w_gate_sh, w_up_sh, w_down_sh, final_g):
    depth = w_in.shape[0]
    assert depth == 1, "the final norm is fused into the last layer; only one layer is supported"
    batch, seq, d = x_prompt.shape
    n_seq, t_new, _ = x_sample.shape
    n_heads = d // HEAD_DIM
    n_c = batch + n_seq
    c_all = jnp.concatenate([c_prompt, c_sample, jnp.zeros((-n_c % 8, d), F32)], axis=0)
    l = 0
    lam_init = 0.8 - 0.6 * math.exp(-0.3 * l)
    lam = (jnp.exp(jnp.sum(lambda_q1[l] * lambda_k1[l])) - jnp.exp(jnp.sum(lambda_q2[l] * lambda_k2[l]))
           + lam_init).astype(F32)
    yp, ys, kp, vp, ks, vs = _layer(
        l, x_prompt, x_sample, cache_k, cache_v, page_table, c_all, w_ada, b_ada, norm1_g, w_in, lam,
        lam_init, subln_g, w_out, norm2_g, w_router, router_bias, w_gate_e, w_up_e, w_down_e,
        w_gate_sh, w_up_sh, w_down_sh, final_g)
    return (yp.reshape(batch, seq, d), ys.reshape(n_seq, t_new, d),
            kp.reshape(1, batch, seq, n_heads, HEAD_DIM), vp.reshape(1, batch, seq, n_heads, HEAD_DIM),
            ks.reshape(1, n_seq, t_new, n_heads, HEAD_DIM), vs.reshape(1, n_seq, t_new, n_heads, HEAD_DIM))
```

```python
import functools
import math

import jax
import jax.numpy as jnp
from jax import lax
from jax.experimental import pallas as pl
from jax.experimental.pallas import tpu as pltpu

F32 = jnp.float32
BF16 = jnp.bfloat16

HEAD_DIM = 128
DIFF_DIM = HEAD_DIM // 2
ROT_DIM = DIFF_DIM // 4
ROT_HALF = ROT_DIM // 2
ROPE_THETA = 500000.0
N_EXPERT_GROUPS = 8
TOPK_GROUPS = 4
TOP_K = 8
ROUTED_SCALE = 2.5
EPS = 1e-6
NEG = -1e30
SB_CHUNK = 256
EXPERT_ROWS = 128
VMEM_LIMIT = 56 * 1024 * 1024


def _params(*sem):
    return pltpu.CompilerParams(dimension_semantics=sem, vmem_limit_bytes=VMEM_LIMIT)


def _mod(ref):
    return ref[0] if len(ref.shape) == 3 else ref[...]


def _rms(x):
    return x * lax.rsqrt(jnp.mean(x * x, axis=-1, keepdims=True) + EPS)


def _adaln_kernel(c_ref, w_ref, b_ref, o_ref):
    c = c_ref[...]
    s = (c * jax.nn.sigmoid(c)).astype(BF16)
    o_ref[...] = jnp.dot(s, w_ref[...].astype(BF16), preferred_element_type=F32) + b_ref[...]


def _adaln(c, w, b):
    m, d = c.shape
    n = w.shape[1]
    tn = _tile(n, 1024, HEAD_DIM)
    return pl.pallas_call(
        _adaln_kernel,
        grid=(n // tn,),
        in_specs=[pl.BlockSpec((m, d), lambda j: (0, 0)),
                  pl.BlockSpec((d, tn), lambda j: (0, j)),
                  pl.BlockSpec((1, tn), lambda j: (0, j))],
        out_specs=pl.BlockSpec((m, tn), lambda j: (0, j)),
        out_shape=jax.ShapeDtypeStruct((m, n), F32),
        compiler_params=_params("arbitrary"),
        name="adaln",
    )(c, w, b.reshape(1, n))


def _norm_mod_kernel(x_ref, g_ref, sh_ref, sc_ref, h_ref):
    y = _rms(x_ref[...]) * g_ref[...]
    h_ref[...] = (y * (1.0 + _mod(sc_ref)) + _mod(sh_ref)).astype(h_ref.dtype)


def _mod_spec(arr, tm, rows_per_group, d):
    if arr.ndim == 3:
        return pl.BlockSpec((1, 1, d), lambda i: (i * tm // rows_per_group, 0, 0))
    return pl.BlockSpec((tm, d), lambda i: (i, 0))


def _norm_mod(x, g, shift, scale, rows_per_group, tm):
    m, d = x.shape
    return pl.pallas_call(
        _norm_mod_kernel,
        grid=(m // tm,),
        in_specs=[pl.BlockSpec((tm, d), lambda i: (i, 0)),
                  pl.BlockSpec((1, d), lambda i: (0, 0)),
                  _mod_spec(shift, tm, rows_per_group, d),
                  _mod_spec(scale, tm, rows_per_group, d)],
        out_specs=pl.BlockSpec((tm, d), lambda i: (i, 0)),
        out_shape=jax.ShapeDtypeStruct((m, d), BF16),
        compiler_params=_params("arbitrary"),
        name="norm_mod",
    )(x, g.reshape(1, d), shift, scale)


def _rope(x, cos, sa, sb):
    return x * cos + pltpu.roll(x, HEAD_DIM - ROT_HALF, 1) * sa + pltpu.roll(x, ROT_HALF, 1) * sb


def _proj_kernel(h_ref, w_ref, cos_ref, sa_ref, sb_ref, *o_refs, rope, diff_scale):
    acc = jnp.dot(h_ref[...], w_ref[...], preferred_element_type=F32)
    j = pl.program_id(0)

    def store(val):
        for o in o_refs:
            o[...] = val.astype(o.dtype)

    @pl.when(j == 0)
    def _():
        if rope:
            cos, sa, sb = cos_ref[...], sa_ref[...], sb_ref[...]
            blocks = [_rope(acc[:, c:c + HEAD_DIM], cos, sa, sb) * diff_scale
                      for c in range(0, acc.shape[1], HEAD_DIM)]
            store(jnp.concatenate(blocks, axis=1))
        else:
            store(acc)

    @pl.when(j != 0)
    def _():
        store(acc)


def _proj(h, w, col0, n, tables, out_dtypes, rope, diff_scale, tm):
    m, d = h.shape
    tn = n // 2
    cos, sa, sb = tables
    n_tab = cos.shape[0] // tm
    tab_spec = pl.BlockSpec((tm, HEAD_DIM), lambda j, i: (i % n_tab, 0))
    off = col0 // tn
    return pl.pallas_call(
        functools.partial(_proj_kernel, rope=rope, diff_scale=diff_scale),
        grid=(2, m // tm),
        in_specs=[pl.BlockSpec((tm, d), lambda j, i: (i, 0)),
                  pl.BlockSpec((d, tn), lambda j, i: (0, off + j)),
                  tab_spec, tab_spec, tab_spec],
        out_specs=[pl.BlockSpec((tm, tn), lambda j, i: (i, j)) for _ in out_dtypes],
        out_shape=[jax.ShapeDtypeStruct((m, n), dt) for dt in out_dtypes],
        compiler_params=_params("arbitrary", "arbitrary"),
        name="proj",
    )(h, w, cos, sa, sb)


def _rope_tables(pos):
    inv_freq = ROPE_THETA ** (-jnp.arange(ROT_HALF, dtype=F32) / ROT_HALF)
    ang = pos.astype(F32)[:, None] * inv_freq[None, :]
    cos, sin = jnp.cos(ang), jnp.sin(ang)
    t = pos.shape[0]
    one = jnp.ones((t, DIFF_DIM - ROT_DIM), F32)
    zero = jnp.zeros((t, DIFF_DIM - ROT_DIM), F32)
    z8 = jnp.zeros((t, ROT_HALF), F32)
    cos_l = jnp.concatenate([cos, cos, one], axis=1)
    sa_l = jnp.concatenate([-sin, z8, zero], axis=1)
    sb_l = jnp.concatenate([z8, sin, zero], axis=1)
    return tuple(jnp.concatenate([a, a], axis=1) for a in (cos_l, sa_l, sb_l))


def _split_maps(q):
    lane = lax.broadcasted_iota(jnp.int32, q.shape, 1)
    zero = jnp.zeros_like(q)
    return jnp.concatenate([jnp.where(lane < DIFF_DIM, q, zero),
                            jnp.where(lane >= DIFF_DIM, q, zero)], axis=0)


def _diff_attn_kernel(lam_ref, q_ref, k_ref, v_ref, g_ref, o_ref, m_sc, l_sc, acc_sc, *, t, out_scale):
    i = pl.program_id(2)
    q2 = _split_maps(q_ref[...])
    m_sc[...] = jnp.full(m_sc.shape, NEG, F32)
    l_sc[...] = jnp.zeros(l_sc.shape, F32)
    acc_sc[...] = jnp.zeros(acc_sc.shape, F32)

    def step(j, masked):
        start = pl.multiple_of(j * t, t)
        kj = k_ref[pl.ds(start, t), :]
        vj = v_ref[pl.ds(start, t), :]
        s = lax.dot_general(q2, kj, (((1,), (1,)), ((), ())), preferred_element_type=F32)
        if masked:
            row = lax.broadcasted_iota(jnp.int32, (2 * t, t), 0)
            row = jnp.where(row >= t, row - t, row)
            col = lax.broadcasted_iota(jnp.int32, (2 * t, t), 1)
            s = jnp.where(col <= row, s, NEG)
        m_old = m_sc[...]
        m_new = jnp.maximum(m_old, jnp.max(s, axis=-1, keepdims=True))
        alpha = jnp.exp(m_old - m_new)
        p = jnp.exp(s - jnp.tile(m_new, (1, t // HEAD_DIM)))
        l_sc[...] = alpha * l_sc[...] + jnp.sum(p, axis=-1, keepdims=True)
        acc_sc[...] = alpha * acc_sc[...] + jnp.dot(p.astype(BF16), vj, preferred_element_type=F32)
        m_sc[...] = m_new

    def body(j, c):
        step(j, False)
        return c

    lax.fori_loop(0, i, body, 0)
    step(i, True)

    o2 = acc_sc[...] / l_sc[...]
    o = o2[:t] - lam_ref[0] * o2[t:]
    o_ref[...] = (_rms(o) * g_ref[...] * out_scale).astype(o_ref.dtype)


def _sb_weights(z, strict):
    soft = jnp.log(1.0 + jnp.exp(-jnp.abs(z)))
    log_beta = jnp.minimum(z, 0.0) - soft
    log_1m = jnp.minimum(-z, 0.0) - soft
    if strict is not None:
        log_1m = jnp.where(strict, log_1m, 0.0)
    return log_beta, log_1m


def _later_sum(x, upper):
    hi = x.astype(BF16)
    lo = (x - hi.astype(F32)).astype(BF16)
    return (jnp.dot(hi, upper, preferred_element_type=F32)
            + jnp.dot(lo, upper, preferred_element_type=F32))


def _upper(t):
    r = lax.broadcasted_iota(jnp.int32, (t, t), 0)
    c = lax.broadcasted_iota(jnp.int32, (t, t), 1)
    return jnp.where(r > c, 1.0, 0.0).astype(BF16)


def _sb_attn_kernel(q_ref, k_ref, v_ref, o_ref, c_sc, acc_sc, *, t, scale):
    i = pl.program_id(2)
    q = q_ref[...]
    ch = min(t, SB_CHUNK)
    upper = _upper(ch)
    c_sc[...] = jnp.zeros(c_sc.shape, F32)
    acc_sc[...] = jnp.zeros(acc_sc.shape, F32)

    def step(j, masked):
        start = pl.multiple_of(j * t, t)
        kj = k_ref[pl.ds(start, t), :]
        vj = v_ref[pl.ds(start, t), :]
        z = lax.dot_general(q, kj, (((1,), (1,)), ((), ())), preferred_element_type=F32) * scale
        strict = None
        if masked:
            row = lax.broadcasted_iota(jnp.int32, (t, t), 0)
            col = lax.broadcasted_iota(jnp.int32, (t, t), 1)
            strict = col < row
        log_beta, log_1m = _sb_weights(z, strict)
        chunks = [log_1m[:, c * ch:(c + 1) * ch] for c in range(t // ch)]
        within = _later_sum(jnp.concatenate(chunks, axis=0), upper)
        later = c_sc[...]
        tails = [None] * len(chunks)
        for c in reversed(range(len(chunks))):
            tails[c] = within[c * t:(c + 1) * t] + jnp.tile(later, (1, ch // HEAD_DIM))
            later = later + jnp.sum(chunks[c], axis=-1, keepdims=True)
        c_sc[...] = later
        a = jnp.exp(log_beta + jnp.concatenate(tails, axis=1))
        if masked:
            a = jnp.where(strict, a, 0.0)
        acc_sc[...] += jnp.dot(a.astype(BF16), vj, preferred_element_type=F32)

    step(i, True)

    def body(r, c):
        step(i - 1 - r, False)
        return c

    lax.fori_loop(0, i, body, 0)
    o_ref[...] = acc_sc[...].astype(o_ref.dtype)


def _prompt_attention(q, k, v, lam, subln_g, batch, seq, n_diff, lam_init, t):
    m, d = q.shape
    n_heads = d // HEAD_DIM
    n_sb = n_heads - n_diff
    nq = seq // t
    params = _params("arbitrary", "arbitrary", "arbitrary")

    def q_spec(h0):
        return pl.BlockSpec((t, HEAD_DIM), lambda b, h, i: (b * nq + i, h0 + h))

    def kv_spec(h0):
        return pl.BlockSpec((seq, HEAD_DIM), lambda b, h, i: (b, h0 + h))

    o_spec = pl.BlockSpec((t, HEAD_DIM), lambda b, h, i: (b * nq + i, h))
    o_diff = pl.pallas_call(
        functools.partial(_diff_attn_kernel, t=t, out_scale=1.0 - lam_init),
        grid=(batch, n_diff, nq),
        in_specs=[pl.BlockSpec(memory_space=pltpu.SMEM), q_spec(0), kv_spec(0), kv_spec(0),
                  pl.BlockSpec((1, HEAD_DIM), lambda b, h, i: (0, 0))],
        out_specs=o_spec,
        out_shape=jax.ShapeDtypeStruct((m, n_diff * HEAD_DIM), BF16),
        scratch_shapes=[pltpu.VMEM((2 * t, HEAD_DIM), F32), pltpu.VMEM((2 * t, HEAD_DIM), F32),
                        pltpu.VMEM((2 * t, HEAD_DIM), F32)],
        compiler_params=params,
        name="diff_attn",
    )(lam, q, k, v, subln_g.reshape(1, HEAD_DIM))
    o_sb = pl.pallas_call(
        functools.partial(_sb_attn_kernel, t=t, scale=HEAD_DIM ** -0.5),
        grid=(batch, n_sb, nq),
        in_specs=[q_spec(n_diff), kv_spec(n_diff), kv_spec(n_diff)],
        out_specs=o_spec,
        out_shape=jax.ShapeDtypeStruct((m, n_sb * HEAD_DIM), BF16),
        scratch_shapes=[pltpu.VMEM((t, HEAD_DIM), F32), pltpu.VMEM((t, HEAD_DIM), F32)],
        compiler_params=params,
        name="sb_attn",
    )(q, k, v)
    return o_diff, o_sb


def _sample_attn_kernel(pt_ref, lam_ref, q_ref, kn_ref, vn_ref, *rest, n_heads, n_diff, t_new, n_par,
                        last_step, sb_scale, out_scale):
    k_refs, v_refs = rest[:n_par], rest[n_par:2 * n_par]
    g_ref, o_ref, qm_sc, m_sc, l_sc, acc_sc = rest[2 * n_par:]
    p = pl.program_id(1)
    half = 4
    rows = 2 * half
    row = lax.broadcasted_iota(jnp.int32, (rows, 1), 0)
    t_row = row % half
    page = k_refs[0].shape[0] // n_heads
    upper = _upper(page)

    def head_rows(ref, h):
        return ref[pl.ds(h, page, stride=n_heads), :].astype(BF16)
    nd8 = rows * n_diff
    ns8 = rows * (n_heads - n_diff)

    def cols(h):
        return slice(h * HEAD_DIM, (h + 1) * HEAD_DIM)

    def hrows(h):
        return slice(h * rows, (h + 1) * rows)

    @pl.when(p == 0)
    def _init():
        pad = jnp.zeros((half - t_new, HEAD_DIM), F32)
        for h in range(n_heads):
            qh = jnp.concatenate([q_ref[0, :, cols(h)], pad], axis=0) if t_new < half else q_ref[0, :, cols(h)]
            if h < n_diff:
                lane = lax.broadcasted_iota(jnp.int32, qh.shape, 1)
                qm = jnp.concatenate([jnp.where(lane < DIFF_DIM, qh, 0.0),
                                      jnp.where(lane >= DIFF_DIM, qh, 0.0)], axis=0)
            else:
                qm = jnp.concatenate([qh, jnp.zeros_like(qh)], axis=0)
            qm_sc[h] = qm
            kn = kn_ref[0, :, cols(h)]
            vn = vn_ref[0, :, cols(h)]
            s = [jnp.sum(qm * kn[j:j + 1, :], axis=-1, keepdims=True) for j in range(t_new)]
            if h < n_diff:
                ok = [t_row >= j for j in range(t_new)]
                m = functools.reduce(jnp.maximum, [jnp.where(ok[j], s[j], NEG) for j in range(t_new)])
                pj = [jnp.where(ok[j], jnp.exp(s[j] - m), 0.0) for j in range(t_new)]
                m_sc[hrows(h)] = m
                l_sc[hrows(h)] = sum(pj)
                acc_sc[hrows(h)] = sum(pj[j] * vn[j:j + 1, :] for j in range(t_new))
            else:
                strict = [t_row > j for j in range(t_new)]
                lb, l1 = zip(*[_sb_weights(s[j] * sb_scale, strict[j]) for j in range(t_new)])
                acc = jnp.zeros((rows, HEAD_DIM), F32)
                later = jnp.zeros((rows, 1), F32)
                for j in reversed(range(t_new)):
                    acc = acc + jnp.where(strict[j], jnp.exp(lb[j] + later), 0.0) * vn[j:j + 1, :]
                    later = later + l1[j]
                m_sc[hrows(h)] = later
                l_sc[hrows(h)] = jnp.ones((rows, 1), F32)
                acc_sc[hrows(h)] = acc

    nt = (((1,), (1,)), ((), ()))
    s = jnp.concatenate(
        [jnp.concatenate([lax.dot_general(qm_sc[h].astype(BF16), head_rows(k_refs[a], h), nt,
                                          preferred_element_type=F32) for h in range(n_heads)], axis=0)
         for a in range(n_par)], axis=1)

    m_old = m_sc[:nd8]
    m_new = jnp.maximum(m_old, jnp.max(s[:nd8], axis=-1, keepdims=True))
    alpha = jnp.exp(m_old - m_new)
    p_diff = jnp.exp(s[:nd8] - m_new)
    l_sc[:nd8] = alpha * l_sc[:nd8] + jnp.sum(p_diff, axis=-1, keepdims=True)
    m_sc[:nd8] = m_new

    log_beta, log_1m = _sb_weights(s[nd8:] * sb_scale, None)
    chunks = [log_1m[:, a * page:(a + 1) * page] for a in range(n_par)]
    within = _later_sum(jnp.concatenate(chunks, axis=0), upper)
    later = m_sc[nd8:]
    tails = [None] * n_par
    for a in reversed(range(n_par)):
        tails[a] = within[a * ns8:(a + 1) * ns8] + later
        later = later + jnp.sum(chunks[a], axis=-1, keepdims=True)
    m_sc[nd8:] = later
    p_sb = jnp.exp(log_beta + jnp.concatenate(tails, axis=1))

    pm = jnp.concatenate([p_diff, p_sb], axis=0)
    pv = []
    for h in range(n_heads):
        parts = [jnp.dot(pm[hrows(h), a * page:(a + 1) * page].astype(BF16),
                         head_rows(v_refs[a], h), preferred_element_type=F32)
                 for a in range(n_par)]
        pv.append(functools.reduce(lambda x, y: x + y, parts))
    decay = jnp.concatenate([alpha, jnp.ones((ns8, 1), F32)], axis=0)
    acc_sc[...] = decay * acc_sc[...] + jnp.concatenate(pv, axis=0)

    @pl.when(p == last_step)
    def _fin():
        for h in range(n_heads):
            if h < n_diff:
                o2 = acc_sc[hrows(h)] / l_sc[hrows(h)]
                o = o2[:half] - lam_ref[0] * o2[half:]
                o = _rms(o) * g_ref[...] * out_scale
            else:
                o = acc_sc[hrows(h)][:half]
            o_ref[0, :, cols(h)] = o[:t_new]


def _sample_attention(l, q, k_new, v_new, cache_k, cache_v, page_table, lam, subln_g, n_diff, lam_init):
    n_seq, t_new, d = q.shape
    n_heads = d // HEAD_DIM
    n_pages = page_table.shape[1]
    n_phys, page = cache_k.shape[1:3]
    assert t_new <= 4
    n_par = _tile(n_pages, 4, 1)
    n_steps = n_pages // n_par
    new_spec = pl.BlockSpec((1, t_new, d), lambda b, p, pt: (b, 0, 0))
    cache_k = cache_k.reshape(-1, HEAD_DIM)
    cache_v = cache_v.reshape(-1, HEAD_DIM)

    def cache_spec(a):
        return pl.BlockSpec((page * n_heads, HEAD_DIM),
                            lambda b, p, pt: (l * n_phys + pt[b, n_pages - (p + 1) * n_par + a], 0))

    cache_specs = [cache_spec(a) for a in range(n_par)]
    rows = 8 * n_heads
    return pl.pallas_call(
        functools.partial(_sample_attn_kernel, n_heads=n_heads, n_diff=n_diff, t_new=t_new, n_par=n_par,
                          last_step=n_steps - 1, sb_scale=HEAD_DIM ** -0.5, out_scale=1.0 - lam_init),
        grid_spec=pltpu.PrefetchScalarGridSpec(
            num_scalar_prefetch=1,
            grid=(n_seq, n_steps),
            in_specs=[pl.BlockSpec(memory_space=pltpu.SMEM), new_spec, new_spec, new_spec,
                      *cache_specs, *cache_specs,
                      pl.BlockSpec((1, HEAD_DIM), lambda b, p, pt: (0, 0))],
            out_specs=new_spec,
            scratch_shapes=[pltpu.VMEM((n_heads, 8, HEAD_DIM), F32), pltpu.VMEM((rows, 1), F32),
                            pltpu.VMEM((rows, 1), F32), pltpu.VMEM((rows, HEAD_DIM), F32)]),
        out_shape=jax.ShapeDtypeStruct((n_seq, t_new, d), F32),
        compiler_params=_params("arbitrary", "arbitrary"),
        name="sample_attn",
    )(page_table, lam, q, k_new, v_new, *([cache_k] * n_par), *([cache_v] * n_par),
      subln_g.reshape(1, HEAD_DIM))


def _split_bf16(x):
    hi = x.astype(BF16)
    return hi, (x - hi.astype(F32)).astype(BF16)


def _outproj_kernel(oa_ref, ob_ref, wa_ref, wb_ref, x_ref, g1_ref, n2_ref, sh_ref, sc_ref,
                    rhi_ref, rlo_ref, x1_ref, h2_ref, lg_ref):
    attn = (jnp.dot(oa_ref[...].astype(BF16), wa_ref[...], preferred_element_type=F32)
            + jnp.dot(ob_ref[...].astype(BF16), wb_ref[...], preferred_element_type=F32))
    x1 = x_ref[...] + _mod(g1_ref) * attn
    x1_ref[...] = x1
    h2 = _rms(x1) * n2_ref[...] * (1.0 + _mod(sc_ref)) + _mod(sh_ref)
    hi, lo = _split_bf16(h2)
    h2_ref[...] = hi.astype(F32)
    lg_ref[...] = (jnp.dot(hi, rhi_ref[...], preferred_element_type=F32)
                   + jnp.dot(lo, rhi_ref[...], preferred_element_type=F32)
                   + jnp.dot(hi, rlo_ref[...], preferred_element_type=F32))


def _outproj(o_a, o_b, blk_b, w_out, x, g1, norm2_g, sh2, sc2, r_hi, r_lo, rows_per_group, tm):
    m, d = x.shape
    dh = d // 2
    n_e = r_hi.shape[1]
    row = lambda i: (i, 0)
    const = lambda i: (0, 0)
    return pl.pallas_call(
        _outproj_kernel,
        grid=(m // tm,),
        in_specs=[pl.BlockSpec((tm, dh), row),
                  pl.BlockSpec((tm, dh), lambda i: (i, blk_b)),
                  pl.BlockSpec((dh, d), const),
                  pl.BlockSpec((dh, d), lambda i: (1, 0)),
                  pl.BlockSpec((tm, d), row),
                  _mod_spec(g1, tm, rows_per_group, d),
                  pl.BlockSpec((1, d), const),
                  _mod_spec(sh2, tm, rows_per_group, d),
                  _mod_spec(sc2, tm, rows_per_group, d),
                  pl.BlockSpec((d, n_e), const),
                  pl.BlockSpec((d, n_e), const)],
        out_specs=[pl.BlockSpec((tm, d), row), pl.BlockSpec((tm, d), row), pl.BlockSpec((tm, n_e), row)],
        out_shape=[jax.ShapeDtypeStruct((m, d), F32), jax.ShapeDtypeStruct((m, d), F32),
                   jax.ShapeDtypeStruct((m, n_e), F32)],
        compiler_params=_params("arbitrary"),
        name="outproj",
    )(o_a, o_b, w_out, w_out, x, g1, norm2_g.reshape(1, d), sh2, sc2, r_hi, r_lo)


def _route_kernel(lg_ref, bias_ref, idx_ref, gate_ref, rank_ref, cnt_ref, cnt_sc):
    @pl.when(pl.program_id(0) == 0)
    def _():
        cnt_sc[...] = jnp.zeros(cnt_sc.shape, F32)

    scores = jax.nn.sigmoid(lg_ref[...]).T
    n_e, tm = scores.shape
    per = n_e // N_EXPERT_GROUPS
    choice = scores + bias_ref[...]
    ninf = -jnp.inf
    sub = lax.broadcasted_iota(jnp.int32, (per, tm), 0).astype(F32)
    grp = []
    for g in range(N_EXPERT_GROUPS):
        cg = choice[g * per:(g + 1) * per]
        m1 = jnp.max(cg, axis=0, keepdims=True)
        i1 = jnp.min(jnp.where(cg == m1, sub, float(per)), axis=0, keepdims=True)
        m2 = jnp.max(jnp.where(sub == i1, ninf, cg), axis=0, keepdims=True)
        grp.append(m1 + m2)
    masked = []
    for g in range(N_EXPERT_GROUPS):
        beaten = sum(((grp[o] > grp[g]) if o > g else (grp[o] >= grp[g])).astype(F32)
                     for o in range(N_EXPERT_GROUPS) if o != g)
        masked.append(jnp.where(beaten < TOPK_GROUPS, choice[g * per:(g + 1) * per], NEG))
    cm = jnp.concatenate(masked, axis=0)
    eid = lax.broadcasted_iota(jnp.int32, (n_e, tm), 0).astype(F32)
    r = lax.broadcasted_iota(jnp.int32, (tm, tm), 0)
    c = lax.broadcasted_iota(jnp.int32, (tm, tm), 1)
    earlier = jnp.where(r < c, 1.0, 0.0).astype(BF16)
    count = cnt_sc[...]
    idx, gate, rank = [], [], []
    for _ in range(TOP_K):
        mx = jnp.max(cm, axis=0, keepdims=True)
        ik = jnp.min(jnp.where(cm == mx, eid, float(n_e)), axis=0, keepdims=True)
        hit = eid == ik
        gate.append(jnp.sum(jnp.where(hit, scores, 0.0), axis=0, keepdims=True))
        cm = jnp.where(hit, ninf, cm)
        idx.append(ik)
        one = jnp.where(hit, 1.0, 0.0)
        before = jnp.dot(one.astype(BF16), earlier, preferred_element_type=F32)
        rank.append(jnp.sum(jnp.where(hit, before + count, 0.0), axis=0, keepdims=True))
        count = count + jnp.sum(one, axis=1, keepdims=True)
    cnt_sc[...] = count
    cnt_ref[...] = count
    gates = jnp.concatenate(gate, axis=0)
    idx_ref[...] = jnp.concatenate(idx, axis=0).astype(jnp.int32)
    rank_ref[...] = jnp.concatenate(rank, axis=0).astype(jnp.int32)
    gate_ref[...] = gates / jnp.sum(gates, axis=0, keepdims=True) * ROUTED_SCALE


def _route(logits, bias, tm):
    m, n_e = logits.shape
    slot = pl.BlockSpec((TOP_K, tm), lambda i: (0, i))
    return pl.pallas_call(
        _route_kernel,
        grid=(m // tm,),
        in_specs=[pl.BlockSpec((tm, n_e), lambda i: (i, 0)),
                  pl.BlockSpec((n_e, 1), lambda i: (0, 0))],
        out_specs=[slot, slot, slot, pl.BlockSpec((n_e, 1), lambda i: (0, 0))],
        out_shape=[jax.ShapeDtypeStruct((TOP_K, m), jnp.int32), jax.ShapeDtypeStruct((TOP_K, m), F32),
                   jax.ShapeDtypeStruct((TOP_K, m), jnp.int32), jax.ShapeDtypeStruct((n_e, 1), F32)],
        scratch_shapes=[pltpu.VMEM((n_e, 1), F32)],
        compiler_params=_params("arbitrary"),
        name="route",
    )(logits, bias.reshape(n_e, 1))


def _dest_kernel(idx_ref, rank_ref, start_ref, dest_ref):
    idx = idx_ref[...]
    k, tm = idx.shape
    n_e = start_ref.shape[0]
    eid = lax.broadcasted_iota(jnp.int32, (n_e, tm), 0)
    base = [jnp.sum(jnp.where(eid == idx[j:j + 1, :], start_ref[...], 0.0), axis=0, keepdims=True)
            for j in range(k)]
    dest_ref[...] = jnp.concatenate(base, axis=0).astype(jnp.int32) + rank_ref[...]


def _dest(idx_t, rank_t, start, tm):
    k, m = idx_t.shape
    n_e = start.shape[0]
    slot = pl.BlockSpec((k, tm), lambda i: (0, i))
    return pl.pallas_call(
        _dest_kernel,
        grid=(m // tm,),
        in_specs=[slot, slot, pl.BlockSpec((n_e, 1), lambda i: (0, 0))],
        out_specs=slot,
        out_shape=jax.ShapeDtypeStruct((k, m), jnp.int32),
        compiler_params=_params("arbitrary"),
        name="dest",
    )(idx_t, rank_t, start)


def _expert_kernel(be_ref, ob_ref, first_ref, next_ref, slot_ref, nv_ref, x_ref, wg_hbm, wu_hbm, wd_hbm,
                   y_ref, wg_buf, wu_buf, wd_buf, wg_sc, wu_sc, wd_sc, sem):
    i = pl.program_id(0)

    def copies(e, slot):
        return (pltpu.make_async_copy(wg_hbm.at[e], wg_buf.at[slot], sem.at[slot, 0]),
                pltpu.make_async_copy(wu_hbm.at[e], wu_buf.at[slot], sem.at[slot, 1]),
                pltpu.make_async_copy(wd_hbm.at[e], wd_buf.at[slot], sem.at[slot, 2]))

    @pl.when(first_ref[i] == 1)
    def _():
        slot = slot_ref[i]

        @pl.when(i == 0)
        def _():
            for cp in copies(be_ref[i], slot):
                cp.start()

        @pl.when(next_ref[i] >= 0)
        def _():
            for cp in copies(next_ref[i], 1 - slot):
                cp.start()

        for cp in copies(be_ref[i], slot):
            cp.wait()

        wg_sc[...] = wg_buf[slot].astype(BF16)
        wu_sc[...] = wu_buf[slot].astype(BF16)
        wd_sc[...] = wd_buf[slot].astype(BF16)

    @pl.when(i < nv_ref[0])
    def _():
        x = x_ref[...].astype(BF16)
        a = jnp.dot(x, wg_sc[...], preferred_element_type=F32)
        u = jnp.dot(x, wu_sc[...], preferred_element_type=F32)
        mid = (a * jax.nn.sigmoid(a) * u).astype(BF16)
        y_ref[...] = jnp.dot(mid, wd_sc[...], preferred_element_type=F32)

    @pl.when(i >= nv_ref[0])
    def _():
        y_ref[...] = jnp.zeros(y_ref.shape, F32)


def _experts(x_rows, plan, w_gate, w_up, w_down):
    r, d = x_rows.shape
    n_e, _, f = w_gate.shape
    tm = EXPERT_ROWS
    n_blocks = r // tm
    rows = lambda i, be, ob, *_: (ob[i], 0)
    hbm = pl.BlockSpec(memory_space=pl.ANY)
    return pl.pallas_call(
        _expert_kernel,
        grid_spec=pltpu.PrefetchScalarGridSpec(
            num_scalar_prefetch=6,
            grid=(n_blocks,),
            in_specs=[pl.BlockSpec((tm, d), rows), hbm, hbm, hbm],
            out_specs=pl.BlockSpec((tm, d), lambda i, *_: (i, 0)),
            scratch_shapes=[pltpu.VMEM((2, d, f), F32), pltpu.VMEM((2, d, f), F32), pltpu.VMEM((2, f, d), F32),
                            pltpu.VMEM((d, f), BF16), pltpu.VMEM((d, f), BF16), pltpu.VMEM((f, d), BF16),
                            pltpu.SemaphoreType.DMA((2, 3))]),
        out_shape=jax.ShapeDtypeStruct((r, d), F32),
        compiler_params=_params("arbitrary"),
        name="experts",
    )(*plan, x_rows, w_gate, w_up, w_down)


def _block_plan(counts, n_slots):
    n_e = counts.shape[0]
    tm = EXPERT_ROWS
    padded = (counts + tm - 1) // tm * tm
    pad_end = jnp.cumsum(padded)
    pad_start = pad_end - padded
    n_blocks = -(-n_slots // tm) + n_e
    blk = jnp.arange(n_blocks, dtype=jnp.int32)
    block_expert = jnp.sum((pad_end[None, :] <= blk[:, None] * tm).astype(jnp.int32), axis=1)
    block_expert = jnp.minimum(block_expert, n_e - 1)
    n_valid = (pad_end[-1] // tm).astype(jnp.int32)
    out_block = jnp.minimum(blk, n_valid)
    prev = jnp.concatenate([jnp.full((1,), -1, jnp.int32), block_expert[:-1]])
    first = ((blk < n_valid) & (block_expert != prev)).astype(jnp.int32)
    used = counts > 0
    ids = jnp.arange(n_e, dtype=jnp.int32)
    later = lax.cummin(jnp.where(used, ids, n_e), axis=0, reverse=True)
    nxt = jnp.concatenate([later[1:], jnp.full((1,), n_e, jnp.int32)])
    nxt = jnp.where(nxt < n_e, nxt, -1)
    ordinal = jnp.cumsum(used.astype(jnp.int32)) - 1
    plan = (block_expert, out_block, first, nxt[block_expert], ordinal[block_expert] % 2, n_valid.reshape(1))
    return pad_start, n_blocks * tm, plan


def _final_kernel(h_ref, wg_ref, wu_ref, wd_ref, yk_ref, gt_ref, x1_ref, g2_ref, fg_ref, y_ref):
    h = h_ref[...].astype(BF16)
    a = jnp.dot(h, wg_ref[...], preferred_element_type=F32)
    u = jnp.dot(h, wu_ref[...], preferred_element_type=F32)
    mid = (a * jax.nn.sigmoid(a) * u).astype(BF16)
    moe = jnp.dot(mid, wd_ref[...], preferred_element_type=F32)
    gt = gt_ref[...]
    for j in range(yk_ref.shape[0]):
        moe = moe + yk_ref[j] * gt[:, j:j + 1]
    x2 = x1_ref[...] + _mod(g2_ref) * moe
    y_ref[...] = _rms(x2) * fg_ref[...]


def _final(h2, wg, wu, wd, y_slots, gates, row0, x1, g2, final_g, rows_per_group, tm):
    m, d = x1.shape
    f = wg.shape[1]
    k = y_slots.shape[0]
    assert row0 % tm == 0
    b0 = row0 // tm
    row = lambda i: (i, 0)
    const = lambda i: (0, 0)
    return pl.pallas_call(
        _final_kernel,
        grid=(m // tm,),
        in_specs=[pl.BlockSpec((tm, d), row), pl.BlockSpec((d, f), const), pl.BlockSpec((d, f), const),
                  pl.BlockSpec((f, d), const), pl.BlockSpec((k, tm, d), lambda i: (0, b0 + i, 0)),
                  pl.BlockSpec((tm, k), lambda i: (b0 + i, 0)), pl.BlockSpec((tm, d), row),
                  _mod_spec(g2, tm, rows_per_group, d), pl.BlockSpec((1, d), const)],
        out_specs=pl.BlockSpec((tm, d), row),
        out_shape=jax.ShapeDtypeStruct((m, d), F32),
        compiler_params=_params("arbitrary"),
        name="final",
    )(h2, wg, wu, wd, y_slots, gates, x1, g2, final_g.reshape(1, d))


def _tile(m, cap, align=8):
    for t in range(min(m, cap) // align * align, 0, -align):
        if m % t == 0:
            return t
    raise ValueError(f"no tile for {m} rows")


def _layer(l, xp, xs, cache_k, cache_v, page_table, c_all, w_ada, b_ada, norm1_g, w_in, lam, lam_init,
           subln_g, w_out, norm2_g, w_router, router_bias, w_gate_e, w_up_e, w_down_e,
           w_gate_sh, w_up_sh, w_down_sh, final_g):
    batch, seq, d = xp.shape
    n_seq, t_new, _ = xs.shape
    n_heads = d // HEAD_DIM
    n_diff = n_heads // 2
    n_p, n_s = batch * seq, n_seq * t_new
    past = page_table.shape[1] * cache_k.shape[2]

    mod = _adaln(c_all, w_ada[l], b_ada[l])
    mod_p = [mod[:batch, None, k * d:(k + 1) * d] for k in range(6)]
    mod_s = [jnp.repeat(mod[batch:batch + n_seq, k * d:(k + 1) * d], t_new, axis=0) for k in range(6)]

    w_in_b = w_in[l].astype(BF16)
    w_out_b = w_out[l].astype(BF16)
    r_hi, r_lo = _split_bf16(w_router[l])
    sh_w = (w_gate_sh[l].astype(BF16), w_up_sh[l].astype(BF16), w_down_sh[l].astype(BF16))
    diff_scale = DIFF_DIM ** -0.5

    groups = ((xp.reshape(n_p, d), mod_p, seq, jnp.arange(seq, dtype=jnp.int32)),
              (xs.reshape(n_s, d), mod_s, 1, past + jnp.arange(n_s, dtype=jnp.int32) % t_new))
    proj_out = []
    for x2d, md, rpg, pos in groups:
        m = x2d.shape[0]
        tm = _tile(m, 512)
        h = _norm_mod(x2d, norm1_g[l], md[0], md[1], rpg, tm)
        tables = _rope_tables(pos)
        q_dt = [BF16] if rpg > 1 else [F32]
        (q,) = _proj(h, w_in_b, 0, d, tables, q_dt, True, diff_scale, tm)
        k32, k16 = _proj(h, w_in_b, d, d, tables, [F32, BF16], True, 1.0, tm)
        v32, v16 = _proj(h, w_in_b, 2 * d, d, tables, [F32, BF16], False, 1.0, tm)
        proj_out.append((q, k32, k16, v32, v16))

    (qp, kp32, kp16, vp32, vp16), (qs, ks32, _, vs32, _) = proj_out
    lam1 = lam.reshape(1)
    o_diff, o_sb = _prompt_attention(qp, kp16, vp16, lam1, subln_g[l], batch, seq, n_diff, lam_init,
                                     _tile(seq, 512, SB_CHUNK))
    o_s = _sample_attention(l, qs.reshape(n_seq, t_new, d), ks32.reshape(n_seq, t_new, d),
                            vs32.reshape(n_seq, t_new, d), cache_k, cache_v, page_table, lam1, subln_g[l],
                            n_diff, lam_init).reshape(n_s, d)

    x1s, h2s, lgs = [], [], []
    for (x2d, md, rpg, _), (oa, ob, blk_b) in zip(groups, ((o_diff, o_sb, 0), (o_s, o_s, 1))):
        tm = _tile(x2d.shape[0], 256)
        x1, h2, lg = _outproj(oa, ob, blk_b, w_out_b, x2d, md[2], norm2_g[l], md[3], md[4], r_hi, r_lo, rpg, tm)
        x1s.append(x1)
        h2s.append(h2)
        lgs.append(lg)

    h2_all = jnp.concatenate(h2s, axis=0)
    lg_all = jnp.concatenate(lgs, axis=0)
    n_all = n_p + n_s
    t_route = _tile(n_all, 256, 128)
    idx_t, gate_t, rank_t, counts = _route(lg_all, router_bias[l], t_route)
    n_e = w_router.shape[-1]
    pad_start, n_rows, plan = _block_plan(counts[:, 0].astype(jnp.int32), TOP_K * n_all)
    dest = _dest(idx_t, rank_t, pad_start.astype(F32)[:, None], t_route)
    tok = jnp.broadcast_to(jnp.arange(n_all, dtype=jnp.int32), (TOP_K, n_all))
    row_tok = jnp.zeros((n_rows,), jnp.int32).at[dest.reshape(-1)].set(tok.reshape(-1))
    be, ob, first, nxt, slot, n_valid = plan
    plan = (be + l * n_e, ob, first, jnp.where(nxt >= 0, nxt + l * n_e, -1), slot, n_valid)
    flat = lambda w: w.reshape((-1,) + w.shape[2:])
    y_rows = _experts(h2_all[row_tok], plan, flat(w_gate_e), flat(w_up_e), flat(w_down_e))
    y_slots = y_rows[dest]
    gates = gate_t.T

    outs = []
    row0 = 0
    for (x2d, md, rpg, _), x1, h2 in zip(groups, x1s, h2s):
        m = x2d.shape[0]
        outs.append(_final(h2, *sh_w, y_slots, gates, row0, x1, md[5], final_g, rpg, _tile(m, 128)))
        row0 += m
    return outs[0], outs[1], kp32, vp32, ks32, vs32


def kernel(x_prompt, x_sample, cache_k, cache_v, page_table, c_prompt, c_sample, w_ada, b_ada, norm1_g, w_in, lambda_q1, lambda_k1, lambda_q2, lambda_k2, subln_g, w_out, norm2_g, w_router, router_bias, w_gate_e, w_up_e, w_down_e, w_gate_sh, w_up_sh, w_down_sh, final_g):
    depth = w_in.shape[0]
    assert depth == 1, "the final norm is fused into the last layer; only one layer is supported"
    batch, seq, d = x_prompt.shape
    n_seq, t_new, _ = x_sample.shape
    n_heads = d // HEAD_DIM
    n_c = batch + n_seq
    c_all = jnp.concatenate([c_prompt, c_sample, jnp.zeros((-n_c % 8, d), F32)], axis=0)
    l = 0
    lam_init = 0.8 - 0.6 * math.exp(-0.3 * l)
    lam = (jnp.exp(jnp.sum(lambda_q1[l] * lambda_k1[l])) - jnp.exp(jnp.sum(lambda_q2[l] * lambda_k2[l]))
           + lam_init).astype(F32)
    yp, ys, kp, vp, ks, vs = _layer(
        l, x_prompt, x_sample, cache_k, cache_v, page_table, c_all, w_ada, b_ada, norm1_g, w_in, lam,
        lam_init, subln_g, w_out, norm2_g, w_router, router_bias, w_gate_e, w_up_e, w_down_e,
        w_gate_sh, w_up_sh, w_down_sh, final_g)
    return (yp.reshape(batch, seq, d), ys.reshape(n_seq, t_new, d),
            kp.reshape(1, batch, seq, n_heads, HEAD_DIM), vp.reshape(1, batch, seq, n_heads, HEAD_DIM),
            ks.reshape(1, n_seq, t_new, n_heads, HEAD_DIM), vs.reshape(1, n_seq, t_new, n_heads, HEAD_DIM))
```
